```python
import math
import jax, jax.numpy as jnp
from jax import lax
import numpy as np

D_MODEL = 1024
BATCH = 2
SEQ = 8192
DEPTH = 4
DEC_BATCH = 32
DEC_SEQ = 8
PAST_LEN = 8192
PAGE_SIZE = 128

N_EVEN = (DEPTH + 1) // 2
N_ODD = DEPTH // 2
HEAD_DIM = 64
H_A = 8
H_IDX = 8
D_IDX = 64
TOPK_MAX = 256
H_B = 4
DH_B = 64
H_C = 16
MOBA_BLOCK = 256
MOBA_TOPB = 3
ROPE_THETA = 10000.0
NORM_EPS = 1e-6
SUBLN_EPS = 1e-5
W_A = H_A * HEAD_DIM
W_B = H_B * 2 * DH_B
W_C = H_C * HEAD_DIM
EVEN_COLS = (W_A, W_A, W_A, W_A, H_IDX * D_IDX, D_IDX, H_IDX, W_B, W_B, W_B, W_B)
E_IN = sum(EVEN_COLS)
C_IN = 4 * W_C
QUERY_ROWS_DSA = 256
QUERY_ROWS_DIFF = 256
QUERY_ROWS_MOBA = 64

kernel_name = 'hybrid_dsa_diff_moba_decode_step'


def _rms_norm(x, g, eps=NORM_EPS):
    xf = x.astype(jnp.float32)
    y = xf * lax.rsqrt(jnp.mean(xf * xf, axis=-1, keepdims=True) + eps)
    return (y * g.astype(jnp.float32)).astype(x.dtype)


def _rope(x, pos):
    half = x.shape[-1] // 2
    inv_freq = jnp.exp(-math.log(ROPE_THETA) * jnp.arange(half, dtype=jnp.float32) / half)
    ang = pos.astype(jnp.float32)[:, None] * inv_freq[None, :]
    bshape = (1, pos.shape[0]) + (1,) * (x.ndim - 3) + (half,)
    cos = jnp.cos(ang).reshape(bshape)
    sin = jnp.sin(ang).reshape(bshape)
    xf = x.astype(jnp.float32)
    x1, x2 = xf[..., :half], xf[..., half:]
    return jnp.concatenate([x1 * cos - x2 * sin, x2 * cos + x1 * sin], axis=-1).astype(x.dtype)


def _chunk(n_q, batch, rows):
    c = max(1, min(n_q, rows // batch))
    while n_q % c:
        c -= 1
    return c


def _map_query_chunks(fn, q_arrays, q_pos, chunk):
    T = q_pos.shape[0]
    n = T // chunk
    xs = tuple(jnp.moveaxis(a.reshape((a.shape[0], n, chunk) + a.shape[2:]), 1, 0) for a in q_arrays)
    out = lax.map(lambda args: fn(*args[0], args[1]), (xs, q_pos.reshape(n, chunk)))
    out = jnp.moveaxis(out, 0, 1)
    return out.reshape((out.shape[0], T) + out.shape[3:])


def _gather_pages(cache, page_table, layer):
    g = cache[page_table, layer]
    return g.reshape((g.shape[0], g.shape[1] * g.shape[2]) + g.shape[3:])


def _dsa_attention(q, k, v, q_idx, k_idx, w_idx, q_pos):
    B, L = k.shape[0], k.shape[1]
    topk = min(TOPK_MAX, L // 4)
    key_pos = jnp.arange(L, dtype=jnp.int32)
    b_ix = jnp.arange(B)[:, None, None]
    scale = HEAD_DIM ** -0.5

    def block(qb, qib, wib, pb):
        rel = jax.nn.relu(jnp.einsum('bqhd,bsd->bqhs', qib, k_idx, preferred_element_type=jnp.float32))
        score = jnp.einsum('bqh,bqhs->bqs', wib.astype(jnp.float32), rel)
        score = jnp.where(key_pos[None, None, :] <= pb[None, :, None], score, -jnp.inf)
        _, sel = lax.top_k(score, topk)
        valid = sel <= pb[None, :, None]
        k_sel = k[b_ix, sel]
        v_sel = v[b_ix, sel]
        s = jnp.einsum('bqhd,bqkhd->bhqk', qb, k_sel, preferred_element_type=jnp.float32) * scale
        s = jnp.where(valid[:, None], s, -jnp.inf)
        p = jax.nn.softmax(s, axis=-1).astype(v.dtype)
        return jnp.einsum('bhqk,bqkhd->bqhd', p, v_sel)

    chunk = _chunk(q.shape[1], B, QUERY_ROWS_DSA)
    return _map_query_chunks(block, (q, q_idx, w_idx), q_pos, chunk)


def _diff_attention(q, k, v, q_pos, lam, lam_init, subln_g):
    B, L = k.shape[0], k.shape[1]
    key_pos = jnp.arange(L, dtype=jnp.int32)
    scale = DH_B ** -0.5

    def block(qb, pb):
        s = jnp.einsum('bqhmd,bshmd->bhmqs', qb, k, preferred_element_type=jnp.float32) * scale
        s = jnp.where(key_pos[None, :] <= pb[:, None], s, -jnp.inf)
        p = jax.nn.softmax(s, axis=-1)
        a = (p[:, :, 0] - lam * p[:, :, 1]).astype(v.dtype)
        return jnp.einsum('bhqs,bshd->bqhd', a, v)

    chunk = _chunk(q.shape[1], B, QUERY_ROWS_DIFF)
    o = _map_query_chunks(block, (q,), q_pos, chunk)
    return _rms_norm(o, subln_g, SUBLN_EPS) * (1.0 - lam_init)


def _moba_attention(q, k, v, q_pos):
    B, L, H, Dh = k.shape
    nb = -(-L // MOBA_BLOCK)
    pad = nb * MOBA_BLOCK - L

    def to_blocks(a):
        a = jnp.pad(a, ((0, 0), (0, pad), (0, 0), (0, 0)))
        return a.reshape(B, nb, MOBA_BLOCK, H, Dh).transpose(0, 3, 1, 2, 4)

    kb, vb = to_blocks(k), to_blocks(v)
    k_mean = jnp.mean(kb.astype(jnp.float32), axis=3)
    n_sel = max(1, min(MOBA_TOPB, nb - 1))
    blk_id = jnp.arange(nb, dtype=jnp.int32)
    b_ix = jnp.arange(B)[:, None, None]
    h_ix = jnp.arange(H)[None, :, None]
    scale = Dh ** -0.5

    def block(qb, pb):
        C = pb.shape[0]
        cur = pb // MOBA_BLOCK
        qh = qb.transpose(0, 2, 1, 3)
        gate = jnp.einsum('bhqd,bhnd->bhqn', qh.astype(jnp.float32), k_mean)
        gate = jnp.where(blk_id[None, :] < cur[:, None], gate, -jnp.inf)
        _, sel = lax.top_k(gate, n_sel)
        sel_ok = sel < cur[:, None]
        blocks = jnp.concatenate([sel, jnp.broadcast_to(cur[:, None], (B, H, C, 1)).astype(sel.dtype)], axis=-1)
        ok = jnp.concatenate([sel_ok, jnp.ones((B, H, C, 1), dtype=bool)], axis=-1)
        flat = blocks.reshape(B, H, C * (n_sel + 1))
        kg = kb[b_ix, h_ix, flat].reshape(B, H, C, n_sel + 1, MOBA_BLOCK, Dh)
        vg = vb[b_ix, h_ix, flat].reshape(B, H, C, n_sel + 1, MOBA_BLOCK, Dh)
        s = jnp.einsum('bhqd,bhqnkd->bhqnk', qh, kg, preferred_element_type=jnp.float32) * scale
        kpos = blocks[..., None] * MOBA_BLOCK + jnp.arange(MOBA_BLOCK, dtype=jnp.int32)
        valid = ok[..., None] & (kpos <= pb[:, None, None])
        s = jnp.where(valid, s, -jnp.inf)
        p = jax.nn.softmax(s.reshape(B, H, C, -1), axis=-1).reshape(s.shape).astype(v.dtype)
        return jnp.einsum('bhqnk,bhqnkd->bqhd', p, vg)

    chunk = _chunk(q.shape[1], B, QUERY_ROWS_MOBA)
    return _map_query_chunks(block, (q,), q_pos, chunk)


def _even_layer(x, pos, past, norm_g, w_in, w_out, lam, subln_g, lam_init):
    B, T, _ = x.shape
    h = _rms_norm(x, norm_g)
    split_points = np.cumsum(EVEN_COLS)[:-1].tolist()
    q_a, k_a, v_a, g_a, q_i, k_i, w_i, q_b, k_b, v_b, g_b = jnp.split(h @ w_in, split_points, axis=-1)
    q_a = _rope(q_a.reshape(B, T, H_A, HEAD_DIM), pos)
    kv_a = jnp.stack([_rope(k_a.reshape(B, T, H_A, HEAD_DIM), pos), v_a.reshape(B, T, H_A, HEAD_DIM)], axis=2)
    q_i = _rope(q_i.reshape(B, T, H_IDX, D_IDX), pos)
    k_i = _rope(k_i, pos)
    q_b = _rope(q_b.reshape(B, T, H_B, 2, DH_B), pos)
    k_b = _rope(k_b.reshape(B, T, H_B, 2, DH_B), pos).reshape(B, T, H_B, 2 * DH_B)
    kv_b = jnp.stack([k_b, v_b.reshape(B, T, H_B, 2 * DH_B)], axis=2)
    if past is None:
        kv_a_all, k_i_all, kv_b_all = kv_a, k_i, kv_b
    else:
        kv_a_all = jnp.concatenate([past[0], kv_a], axis=1)
        k_i_all = jnp.concatenate([past[1], k_i], axis=1)
        kv_b_all = jnp.concatenate([past[2], kv_b], axis=1)
    L = kv_a_all.shape[1]
    o_a = _dsa_attention(q_a, kv_a_all[:, :, 0], kv_a_all[:, :, 1], q_i, k_i_all, w_i, pos)
    lf = lam.astype(jnp.float32)
    lam_val = jnp.exp(jnp.sum(lf[0] * lf[1])) - jnp.exp(jnp.sum(lf[2] * lf[3])) + lam_init
    o_b = _diff_attention(q_b, kv_b_all[:, :, 0].reshape(B, L, H_B, 2, DH_B), kv_b_all[:, :, 1],
                          pos, lam_val, lam_init, subln_g)
    mixed = jnp.concatenate([o_a.reshape(B, T, W_A) * jax.nn.silu(g_a),
                             o_b.reshape(B, T, W_B) * jax.nn.silu(g_b)], axis=-1)
    return x + mixed @ w_out, kv_a, k_i, kv_b


def _odd_layer(x, pos, past, norm_g, w_in, w_out):
    B, T, _ = x.shape
    h = _rms_norm(x, norm_g)
    q, k, v, g = jnp.split(h @ w_in, 4, axis=-1)
    q = _rope(q.reshape(B, T, H_C, HEAD_DIM), pos)
    kv = jnp.stack([_rope(k.reshape(B, T, H_C, HEAD_DIM), pos), v.reshape(B, T, H_C, HEAD_DIM)], axis=2)
    kv_all = kv if past is None else jnp.concatenate([past, kv], axis=1)
    o = _moba_attention(q, kv_all[:, :, 0], kv_all[:, :, 1], pos)
    return x + (o.reshape(B, T, W_C) * jax.nn.silu(g)) @ w_out, kv


def _trunk(x, pos, cache_a_kv, cache_a_idx, cache_b_kv, cache_c_kv, page_table,
           norm_g, w_in_even, w_out_even, lam_even, subln_g_even, w_in_odd, w_out_odd, final_norm_g):
    new_a_kv, new_a_idx, new_b_kv, new_c_kv = [], [], [], []
    for li in range(DEPTH):
        j = li // 2
        if li % 2 == 0:
            past = None if page_table is None else (
                _gather_pages(cache_a_kv, page_table, j),
                _gather_pages(cache_a_idx, page_table, j),
                _gather_pages(cache_b_kv, page_table, j))
            lam_init = 0.8 - 0.6 * math.exp(-0.3 * li)
            x, kv_a, k_i, kv_b = _even_layer(x, pos, past, norm_g[li], w_in_even[j], w_out_even[j],
                                             lam_even[j], subln_g_even[j], lam_init)
            new_a_kv.append(kv_a)
            new_a_idx.append(k_i)
            new_b_kv.append(kv_b)
        else:
            past = None if page_table is None else _gather_pages(cache_c_kv, page_table, j)
            x, kv_c = _odd_layer(x, pos, past, norm_g[li], w_in_odd[j], w_out_odd[j])
            new_c_kv.append(kv_c)
    y = _rms_norm(x, final_norm_g)
    return (y, jnp.stack(new_a_kv, axis=1), jnp.stack(new_a_idx, axis=1),
            jnp.stack(new_b_kv, axis=1), jnp.stack(new_c_kv, axis=1))


def setup_inputs(seed: int = 0) -> dict:
    key = jax.random.key(seed)
    ks = jax.random.split(key, 16)
    f32 = jnp.float32
    n_pages = PAST_LEN // PAGE_SIZE
    n_pool = (5 * DEC_BATCH * n_pages + 3) // 4
    x_prompt = jax.random.normal(ks[0], (BATCH, SEQ, D_MODEL), f32)
    x_sample = jax.random.normal(ks[1], (DEC_BATCH, DEC_SEQ, D_MODEL), f32)
    cache_a_kv = jax.random.normal(ks[2], (n_pool, N_EVEN, PAGE_SIZE, 2, H_A, HEAD_DIM), f32)
    cache_a_idx = jax.random.normal(ks[3], (n_pool, N_EVEN, PAGE_SIZE, D_IDX), f32)
    cache_b_kv = jax.random.normal(ks[4], (n_pool, N_EVEN, PAGE_SIZE, 2, H_B, 2 * DH_B), f32)
    cache_c_kv = jax.random.normal(ks[5], (n_pool, N_ODD, PAGE_SIZE, 2, H_C, HEAD_DIM), f32)
    page_table = jax.random.permutation(ks[6], n_pool)[:DEC_BATCH * n_pages].reshape(DEC_BATCH, n_pages).astype(jnp.int32)
    norm_g = 1.0 + 0.02 * jax.random.normal(ks[7], (DEPTH, D_MODEL), f32)
    w_in_even = jax.random.normal(ks[8], (N_EVEN, D_MODEL, E_IN), f32) * D_MODEL ** -0.5
    w_out_even = jax.random.normal(ks[9], (N_EVEN, W_A + W_B, D_MODEL), f32) * (W_A + W_B) ** -0.5
    lam_even = 0.1 * jax.random.normal(ks[10], (N_EVEN, 4, DH_B), f32)
    subln_g_even = 1.0 + 0.02 * jax.random.normal(ks[11], (N_EVEN, 2 * DH_B), f32)
    w_in_odd = jax.random.normal(ks[12], (N_ODD, D_MODEL, C_IN), f32) * D_MODEL ** -0.5
    w_out_odd = jax.random.normal(ks[13], (N_ODD, W_C, D_MODEL), f32) * W_C ** -0.5
    final_norm_g = 1.0 + 0.02 * jax.random.normal(ks[14], (D_MODEL,), f32)
    return {'x_prompt': x_prompt, 'x_sample': x_sample,
            'cache_a_kv': cache_a_kv, 'cache_a_idx': cache_a_idx, 'cache_b_kv': cache_b_kv, 'cache_c_kv': cache_c_kv,
            'page_table': page_table, 'norm_g': norm_g,
            'w_in_even': w_in_even, 'w_out_even': w_out_even, 'lam_even': lam_even, 'subln_g_even': subln_g_even,
            'w_in_odd': w_in_odd, 'w_out_odd': w_out_odd, 'final_norm_g': final_norm_g}


def reference(x_prompt, x_sample, cache_a_kv, cache_a_idx, cache_b_kv, cache_c_kv, page_table, norm_g,
              w_in_even, w_out_even, lam_even, subln_g_even, w_in_odd, w_out_odd, final_norm_g):
    weights = (norm_g, w_in_even, w_out_even, lam_even, subln_g_even, w_in_odd, w_out_odd, final_norm_g)
    pos_prompt = jnp.arange(x_prompt.shape[1], dtype=jnp.int32)
    y_prompt, a_kv_p, a_idx_p, b_kv_p, c_kv_p = _trunk(
        x_prompt, pos_prompt, cache_a_kv, cache_a_idx, cache_b_kv, cache_c_kv, None, *weights)
    past_len = page_table.shape[1] * PAGE_SIZE
    pos_sample = past_len + jnp.arange(x_sample.shape[1], dtype=jnp.int32)
    y_sample, a_kv_s, a_idx_s, b_kv_s, c_kv_s = _trunk(
        x_sample, pos_sample, cache_a_kv, cache_a_idx, cache_b_kv, cache_c_kv, page_table, *weights)
    return (y_prompt, y_sample, a_kv_p, a_idx_p, b_kv_p, c_kv_p, a_kv_s, a_idx_s, b_kv_s, c_kv_s)
```

```python
import functools
import math

import jax
import jax.numpy as jnp
from jax import lax
from jax.experimental import pallas as pl
from jax.experimental.pallas import tpu as pltpu

F32 = jnp.float32
BF16 = jnp.bfloat16
I32 = jnp.int32

HEAD_DIM = 64
H_A = 8
H_IDX = 8
D_IDX = 64
TOPK_MAX = 256
H_B = 4
DH_B = 64
H_C = 16
MOBA_BLOCK = 256
MOBA_TOPB = 3
ROPE_THETA = 10000.0
NORM_EPS = 1e-6
SUBLN_EPS = 1e-5
W_A = H_A * HEAD_DIM
W_B = H_B * 2 * DH_B
W_C = H_C * HEAD_DIM
IDX_COLS = H_IDX * D_IDX + D_IDX + H_IDX
IDX_PAD = 640
LANES = 128
NEG = -1e30
INT_MIN = -2 ** 31
VMEM_LIMIT = 56 * 1024 * 1024


def _cparams(*sem):
    return pltpu.CompilerParams(dimension_semantics=sem, vmem_limit_bytes=VMEM_LIMIT)


def _resident(block_shape, index_map):
    return pl.BlockSpec(block_shape, index_map, pipeline_mode=pl.Buffered(1))


def _rms(x, g, eps):
    ms = jnp.mean(x * x, axis=-1, keepdims=True)
    return x * lax.rsqrt(ms + eps) * g


def _silu(g):
    return g / (1.0 + jnp.exp(-g))


def _rope128(xc, cos, sin, first_half):
    sw = jnp.where(first_half, pltpu.roll(xc, 96, 1), pltpu.roll(xc, 32, 1))
    return xc * cos + sw * sin


def _rope_wide(y, cos, sin, first_half):
    return jnp.concatenate(
        [_rope128(y[:, c * LANES:(c + 1) * LANES], cos, sin, first_half) for c in range(y.shape[1] // LANES)],
        axis=1)


def _first_half_mask(rows):
    lane = lax.broadcasted_iota(I32, (rows, LANES), 1)
    return (lane % HEAD_DIM) < (HEAD_DIM // 2)


def _flash_tile(q, k, v, bias, m, l, acc):
    s = lax.dot_general(q, k, (((1,), (1,)), ((), ())), preferred_element_type=F32)
    if bias is not None:
        s = s + bias
    m_new = jnp.maximum(m, jnp.max(s, axis=1, keepdims=True))
    alpha = jnp.exp(m - m_new)
    p = jnp.exp(s - m_new)
    l = alpha * l + jnp.sum(p, axis=1, keepdims=True)
    acc = alpha * acc + jnp.dot(p.astype(BF16), v, preferred_element_type=F32)
    return m_new, l, acc


def _causal_bias(rows, cols):
    r = lax.broadcasted_iota(I32, (rows, cols), 0)
    c = lax.broadcasted_iota(I32, (rows, cols), 1)
    return jnp.where(c <= r, 0.0, NEG).astype(F32)


def _sortable_key(score):
    bits = lax.bitcast_convert_type(score, I32)
    return jnp.where(bits < 0, bits ^ jnp.int32(0x7FFFFFFF), bits)


def _even_proj_kernel(x_ref, g_ref, wa_ref, wi_ref, wb_ref, cos_ref, sin_ref,
                      qa_ref, kva_ref, ka_ref, va_ref, sg_ref, qi_ref, kif_ref, kib_ref, wio_ref,
                      qb_ref, kvb_ref, kb_ref, vb_ref):
    h = _rms(x_ref[...], g_ref[...], NORM_EPS).astype(BF16)
    cos = cos_ref[...]
    sin = sin_ref[...]
    fh = _first_half_mask(h.shape[0])
    scale = HEAD_DIM ** -0.5

    def proj(w_ref, lo, hi):
        return jnp.dot(h, w_ref[:, lo:hi], preferred_element_type=F32)

    for w_ref, q_ref, kv_ref, k_ref, v_ref, g_lo in ((wa_ref, qa_ref, kva_ref, ka_ref, va_ref, 0),
                                                      (wb_ref, qb_ref, kvb_ref, kb_ref, vb_ref, W_A)):
        q = _rope_wide(proj(w_ref, 0, 512), cos, sin, fh)
        q_ref[...] = (q * scale).astype(BF16)
        k = _rope_wide(proj(w_ref, 512, 1024), cos, sin, fh)
        v = proj(w_ref, 1024, 1536)
        kv_ref[:, 0:512] = k
        kv_ref[:, 512:1024] = v
        k_ref[...] = k.astype(BF16)
        v_ref[...] = v.astype(BF16)
        sg_ref[:, g_lo:g_lo + 512] = _silu(proj(w_ref, 1536, 2048))

    qi_ref[...] = _rope_wide(proj(wi_ref, 0, 512), cos, sin, fh).astype(BF16)
    tail = proj(wi_ref, 512, IDX_PAD)
    ki = _rope128(tail, cos, sin, fh)[:, 0:D_IDX]
    kif_ref[...] = ki
    kib_ref[...] = ki.astype(BF16)
    wio_ref[...] = tail[:, D_IDX:D_IDX + H_IDX]


def _even_proj(x2, g, wa, wi, wb, cos, sin, tm):
    M, D = x2.shape
    row = lambda i: (i, 0)
    const = lambda i: (0, 0)
    out_cols = ((512, BF16), (1024, F32), (512, BF16), (512, BF16), (1024, F32), (512, BF16), (D_IDX, F32),
                (D_IDX, BF16), (H_IDX, F32), (512, BF16), (1024, F32), (512, BF16), (512, BF16))
    return pl.pallas_call(
        _even_proj_kernel,
        grid=(M // tm,),
        in_specs=[pl.BlockSpec((tm, D), row), pl.BlockSpec((1, D), const),
                  _resident(wa.shape, const), _resident(wi.shape, const), _resident(wb.shape, const),
                  pl.BlockSpec((tm, LANES), row), pl.BlockSpec((tm, LANES), row)],
        out_specs=[pl.BlockSpec((tm, c), row) for c, _ in out_cols],
        out_shape=[jax.ShapeDtypeStruct((M, c), dt) for c, dt in out_cols],
        compiler_params=_cparams("parallel"),
        name="even_proj",
    )(x2, g, wa, wi, wb, cos, sin)


def _odd_proj_kernel(x_ref, g_ref, w_ref, cos_ref, sin_ref, q_ref, kv_ref, k_ref, v_ref, sg_ref, ksum_ref):
    h = _rms(x_ref[...], g_ref[...], NORM_EPS).astype(BF16)
    cos = cos_ref[...]
    sin = sin_ref[...]
    fh = _first_half_mask(h.shape[0])
    scale = HEAD_DIM ** -0.5
    for c in range(W_C // 512):
        lo = c * 512
        q = _rope_wide(jnp.dot(h, w_ref[:, lo:lo + 512], preferred_element_type=F32), cos, sin, fh)
        q_ref[:, lo:lo + 512] = (q * scale).astype(BF16)
        k = _rope_wide(jnp.dot(h, w_ref[:, W_C + lo:W_C + lo + 512], preferred_element_type=F32), cos, sin, fh)
        kv_ref[:, lo:lo + 512] = k
        k_ref[:, lo:lo + 512] = k.astype(BF16)
        ksum_ref[:, lo:lo + 512] = jnp.sum(k, axis=0, keepdims=True)
        v = jnp.dot(h, w_ref[:, 2 * W_C + lo:2 * W_C + lo + 512], preferred_element_type=F32)
        kv_ref[:, W_C + lo:W_C + lo + 512] = v
        v_ref[:, lo:lo + 512] = v.astype(BF16)
        sg_ref[:, lo:lo + 512] = _silu(jnp.dot(h, w_ref[:, 3 * W_C + lo:3 * W_C + lo + 512],
                                               preferred_element_type=F32))


def _odd_proj(x2, g, w, cos, sin, tm):
    M, D = x2.shape
    row = lambda i: (i, 0)
    const = lambda i: (0, 0)
    out_cols = ((W_C, BF16), (2 * W_C, F32), (W_C, BF16), (W_C, BF16), (W_C, F32))
    return pl.pallas_call(
        _odd_proj_kernel,
        grid=(M // tm,),
        in_specs=[pl.BlockSpec((tm, D), row), pl.BlockSpec((1, D), const), _resident(w.shape, const),
                  pl.BlockSpec((tm, LANES), row), pl.BlockSpec((tm, LANES), row)],
        out_specs=[pl.BlockSpec((tm, c), row) for c, _ in out_cols]
                  + [pl.BlockSpec((None, 1, W_C), lambda i: (i, 0, 0))],
        out_shape=[jax.ShapeDtypeStruct((M, c), dt) for c, dt in out_cols]
                  + [jax.ShapeDtypeStruct((M // tm, 1, W_C), F32)],
        compiler_params=_cparams("parallel"),
        name="odd_proj",
    )(x2, g, w, cos, sin)


def _out_proj_kernel(*refs, n_in, final):
    m_refs = refs[:n_in]
    w_ref, x_ref = refs[n_in], refs[n_in + 1]
    acc = x_ref[...]
    lo = 0
    for m_ref in m_refs:
        kk = m_ref.shape[1]
        acc = acc + jnp.dot(m_ref[...], w_ref[lo:lo + kk, :], preferred_element_type=F32)
        lo += kk
    if final:
        gf_ref, o_ref, y_ref = refs[n_in + 2:]
        y_ref[...] = _rms(acc, gf_ref[...], NORM_EPS)
    else:
        o_ref = refs[n_in + 2]
    o_ref[...] = acc


def _out_proj(mixed, w, x2, gf, tm):
    M, D = x2.shape
    row = lambda i: (i, 0)
    const = lambda i: (0, 0)
    final = gf is not None
    in_specs = [pl.BlockSpec((tm, m.shape[1]), row) for m in mixed]
    in_specs += [_resident(w.shape, const), pl.BlockSpec((tm, D), row)]
    args = list(mixed) + [w, x2]
    out_specs = [pl.BlockSpec((tm, D), row)]
    out_shape = [jax.ShapeDtypeStruct((M, D), F32)]
    if final:
        in_specs.append(pl.BlockSpec((1, D), const))
        args.append(gf)
        out_specs.append(pl.BlockSpec((tm, D), row))
        out_shape.append(jax.ShapeDtypeStruct((M, D), F32))
    return pl.pallas_call(
        functools.partial(_out_proj_kernel, n_in=len(mixed), final=final),
        grid=(M // tm,), in_specs=in_specs, out_specs=out_specs, out_shape=out_shape,
        compiler_params=_cparams("parallel"),
        name="out_proj",
    )(*args)


def _topk_select(key_ref, thr_ref, tie_ref, nch, rows, topk, col_bits, rb):
    for r0 in range(0, rows, rb):
        def count(pred):
            def body(c, acc):
                return acc + pred(key_ref[c, r0:r0 + rb, :], c).astype(I32)
            acc = lax.fori_loop(0, nch, body, jnp.zeros((rb, LANES), I32))
            return jnp.sum(acc, axis=1, keepdims=True)

        def count_ge(cand):
            cb = jnp.broadcast_to(cand, (rb, LANES))
            return count(lambda blk, c: blk >= cb)

        t0 = jnp.full((rb, 1), INT_MIN, I32)
        t0 = jnp.where(count_ge(jnp.zeros((rb, 1), I32)) >= topk, 0, t0)

        def bit_step(i, t):
            cand = t | jnp.left_shift(jnp.int32(1), 30 - i)
            return jnp.where(count_ge(cand) >= topk, cand, t)

        t = lax.fori_loop(0, 31, bit_step, t0)
        tb = jnp.broadcast_to(t, (rb, LANES))
        n_gt = count(lambda blk, c: blk > tb)
        n_ge = count(lambda blk, c: blk >= tb)
        need = topk - n_gt
        thr_ref[r0:r0 + rb, :] = tb
        tie_ref[r0:r0 + rb, :] = jnp.full((rb, LANES), 2 ** 30, I32)

        @pl.when(jnp.max(n_ge) > topk)
        def _():
            lane = lax.broadcasted_iota(I32, (rb, LANES), 1)

            def tie_step(i, vmax):
                cand = vmax | jnp.left_shift(jnp.int32(1), col_bits - 1 - i)
                cb = jnp.broadcast_to(cand, (rb, LANES))
                below = count(lambda blk, c: (blk == tb) & (c * LANES + lane < cb))
                return jnp.where(below < need, cand, vmax)

            tie = lax.fori_loop(0, col_bits, tie_step, jnp.zeros((rb, 1), I32))
            tie = jnp.where(n_ge > topk, tie, 2 ** 30)
            tie_ref[r0:r0 + rb, :] = jnp.broadcast_to(tie, (rb, LANES))


def _select_bias(key, thr, tie, col):
    sel = (key > thr) | ((key == thr) & (col <= tie) & (key != INT_MIN))
    return jnp.where(sel, 0.0, NEG).astype(F32)


def _dsa_prompt_kernel(qa_ref, qi_ref, wi_ref, ki_ref, ka_ref, va_ref, sg_ref, o_ref,
                       key_ref, thr_ref, tie_ref, *, tq, topk, col_bits):
    i = pl.program_id(1)
    nkt = i + 1
    cpt = tq // LANES
    row = lax.broadcasted_iota(I32, (tq, tq), 0)
    colt = lax.broadcasted_iota(I32, (tq, tq), 1)
    wi = wi_ref[...]

    def score_tile(kt, carry):
        ki = ki_ref[pl.ds(pl.multiple_of(kt * tq, tq), tq), :]
        score = jnp.zeros((tq, tq), F32)
        for h in range(H_IDX):
            rel = lax.dot_general(qi_ref[:, h * D_IDX:(h + 1) * D_IDX], ki, (((1,), (1,)), ((), ())),
                                  preferred_element_type=F32)
            score = score + wi[:, h:h + 1] * jnp.maximum(rel, 0.0)
        key = _sortable_key(score)
        key = jnp.where((kt < i) | (colt <= row), key, INT_MIN)
        for c in range(cpt):
            key_ref[kt * cpt + c] = key[:, c * LANES:(c + 1) * LANES]
        return carry

    lax.fori_loop(0, nkt, score_tile, 0)
    _topk_select(key_ref, thr_ref, tie_ref, nkt * cpt, tq, topk, col_bits, rb=64)

    thr = thr_ref[...]
    tie = tie_ref[...]
    lane = lax.broadcasted_iota(I32, (tq, LANES), 1)

    def bias_chunk(c, carry):
        b = _select_bias(key_ref[c], thr, tie, c * LANES + lane)
        key_ref[c] = lax.bitcast_convert_type(b, I32)
        return carry

    lax.fori_loop(0, nkt * cpt, bias_chunk, 0)

    for h in range(H_A):
        q = qa_ref[:, h * HEAD_DIM:(h + 1) * HEAD_DIM]

        def kv_tile(kt, carry):
            start = pl.multiple_of(kt * tq, tq)
            k = ka_ref[pl.ds(start, tq), h * HEAD_DIM:(h + 1) * HEAD_DIM]
            v = va_ref[pl.ds(start, tq), h * HEAD_DIM:(h + 1) * HEAD_DIM]
            bias = jnp.concatenate(
                [lax.bitcast_convert_type(key_ref[kt * cpt + c], F32) for c in range(cpt)], axis=1)
            return _flash_tile(q, k, v, bias, *carry)

        init = (jnp.full((tq, 1), NEG, F32), jnp.zeros((tq, 1), F32), jnp.zeros((tq, HEAD_DIM), F32))
        m, l, acc = lax.fori_loop(0, nkt, kv_tile, init)
        o = acc / l * sg_ref[:, h * HEAD_DIM:(h + 1) * HEAD_DIM]
        o_ref[:, h * HEAD_DIM:(h + 1) * HEAD_DIM] = o.astype(BF16)


def _dsa_prompt(qa, qi, wi, ki, ka, va, sg, tq):
    B, T, _ = qa.shape
    L = ka.shape[1]
    topk = min(TOPK_MAX, L // 4)
    assert T == L and T % tq == 0 and tq % LANES == 0
    qmap = lambda b, i: (b, i, 0)
    kmap = lambda b, i: (b, 0, 0)
    return pl.pallas_call(
        functools.partial(_dsa_prompt_kernel, tq=tq, topk=topk, col_bits=max(1, (L - 1).bit_length())),
        grid=(B, T // tq),
        in_specs=[pl.BlockSpec((None, tq, W_A), qmap), pl.BlockSpec((None, tq, H_IDX * D_IDX), qmap),
                  pl.BlockSpec((None, tq, H_IDX), qmap),
                  _resident((None, L, D_IDX), kmap), _resident((None, L, W_A), kmap),
                  _resident((None, L, W_A), kmap),
                  pl.BlockSpec((None, tq, W_A), qmap)],
        out_specs=pl.BlockSpec((None, tq, W_A), qmap),
        out_shape=jax.ShapeDtypeStruct((B, T, W_A), BF16),
        scratch_shapes=[pltpu.VMEM((L // LANES, tq, LANES), I32), pltpu.VMEM((tq, LANES), I32),
                        pltpu.VMEM((tq, LANES), I32)],
        compiler_params=_cparams("parallel", "arbitrary"),
        name="dsa_prompt",
    )(qa, qi, wi, ki, ka, va, sg)


def _lambda_value(lam_ref, lam_init):
    lam = lam_ref[...]
    s1 = jnp.sum(lam[0:1] * lam[1:2], axis=1, keepdims=True)
    s2 = jnp.sum(lam[2:3] * lam[3:4], axis=1, keepdims=True)
    return jnp.exp(s1) - jnp.exp(s2) + lam_init


def _diff_finish(acc1, l1, acc2, l2, lam_val, subg, sg, lam_init):
    o = acc1 / l1 - lam_val * (acc2 / l2)
    return _rms(o, subg, SUBLN_EPS) * (1.0 - lam_init) * sg


def _diff_prompt_kernel(q_ref, k_ref, v_ref, sg_ref, lam_ref, subg_ref, o_ref, *, tq, lam_init):
    i = pl.program_id(1)
    lam_val = _lambda_value(lam_ref, lam_init)
    diag = _causal_bias(tq, tq)
    dv = 2 * DH_B
    for h in range(H_B):
        outs = []
        for mp in range(2):
            c0 = (2 * h + mp) * DH_B
            q = q_ref[:, c0:c0 + DH_B]

            def tile(kt, carry, bias=None):
                start = pl.multiple_of(kt * tq, tq)
                return _flash_tile(q, k_ref[pl.ds(start, tq), c0:c0 + DH_B],
                                   v_ref[pl.ds(start, tq), h * dv:(h + 1) * dv], bias, *carry)

            init = (jnp.full((tq, 1), NEG, F32), jnp.zeros((tq, 1), F32), jnp.zeros((tq, dv), F32))
            carry = lax.fori_loop(0, i, tile, init)
            outs.append(tile(i, carry, diag))
        (_, l1, a1), (_, l2, a2) = outs
        o = _diff_finish(a1, l1, a2, l2, lam_val, subg_ref[...], sg_ref[:, h * dv:(h + 1) * dv], lam_init)
        o_ref[:, h * dv:(h + 1) * dv] = o.astype(BF16)


def _diff_prompt(qb, kb, vb, sg, lam, subg, lam_init, tq):
    B, T, _ = qb.shape
    L = kb.shape[1]
    assert T == L and T % tq == 0
    qmap = lambda b, i: (b, i, 0)
    kmap = lambda b, i: (b, 0, 0)
    const = lambda b, i: (0, 0)
    return pl.pallas_call(
        functools.partial(_diff_prompt_kernel, tq=tq, lam_init=lam_init),
        grid=(B, T // tq),
        in_specs=[pl.BlockSpec((None, tq, W_B), qmap), _resident((None, L, W_B), kmap),
                  _resident((None, L, W_B), kmap), pl.BlockSpec((None, tq, W_B), qmap),
                  pl.BlockSpec(lam.shape, const), pl.BlockSpec(subg.shape, const)],
        out_specs=pl.BlockSpec((None, tq, W_B), qmap),
        out_shape=jax.ShapeDtypeStruct((B, T, W_B), BF16),
        compiler_params=_cparams("parallel", "arbitrary"),
        name="diff_prompt",
    )(qb, kb, vb, sg, lam, subg)


def _top_blocks(gate, n_valid, n_sel):
    lane = lax.broadcasted_iota(I32, gate.shape, 1)
    g = jnp.where(lane < n_valid, gate, -jnp.inf)
    sel = jnp.zeros(gate.shape, F32)
    for _ in range(n_sel):
        mx = jnp.max(g, axis=1, keepdims=True)
        first = jnp.min(jnp.where(g == mx, lane, LANES), axis=1, keepdims=True)
        pick = (lane == first) & (lane < n_valid)
        sel = jnp.where(pick, 1.0, sel)
        g = jnp.where(lane == first, -jnp.inf, g)
    return sel


def _moba_prompt_kernel(q_ref, kmean_ref, k_ref, v_ref, sg_ref, o_ref, *, tq, n_sel, hg):
    i = pl.program_id(2)
    diag = _causal_bias(tq, tq)
    blk_row = lax.broadcasted_iota(I32, (LANES, tq), 0)
    for h in range(hg):
        c0 = h * HEAD_DIM
        q = q_ref[:, c0:c0 + HEAD_DIM]
        gate = lax.dot_general(q, kmean_ref[:, c0:c0 + HEAD_DIM].astype(BF16), (((1,), (1,)), ((), ())),
                               preferred_element_type=F32)
        sel = _top_blocks(gate, i, n_sel).astype(BF16)

        def tile(kt, carry):
            start = pl.multiple_of(kt * tq, tq)
            onehot = jnp.where(blk_row == kt, 1.0, 0.0).astype(BF16)
            picked = jnp.dot(sel, onehot, preferred_element_type=F32)
            bias = (picked - 1.0) * (-NEG)
            return _flash_tile(q, k_ref[pl.ds(start, tq), c0:c0 + HEAD_DIM],
                               v_ref[pl.ds(start, tq), c0:c0 + HEAD_DIM], bias, *carry)

        init = (jnp.full((tq, 1), NEG, F32), jnp.zeros((tq, 1), F32), jnp.zeros((tq, HEAD_DIM), F32))
        carry = lax.fori_loop(0, i, tile, init)
        start = pl.multiple_of(i * tq, tq)
        _, l, acc = _flash_tile(q, k_ref[pl.ds(start, tq), c0:c0 + HEAD_DIM],
                                v_ref[pl.ds(start, tq), c0:c0 + HEAD_DIM], diag, *carry)
        o_ref[:, c0:c0 + HEAD_DIM] = (acc / l * sg_ref[:, c0:c0 + HEAD_DIM]).astype(BF16)


def _moba_prompt(q, kmean, k, v, sg, hg=4):
    B, T, _ = q.shape
    L = k.shape[1]
    tq = MOBA_BLOCK
    nb = L // tq
    assert T == L and L % tq == 0 and nb <= LANES
    n_sel = max(1, min(MOBA_TOPB, nb - 1))
    wg = hg * HEAD_DIM
    qmap = lambda b, g, i: (b, i, g)
    kmap = lambda b, g, i: (b, 0, g)
    return pl.pallas_call(
        functools.partial(_moba_prompt_kernel, tq=tq, n_sel=n_sel, hg=hg),
        grid=(B, W_C // wg, T // tq),
        in_specs=[pl.BlockSpec((None, tq, wg), qmap), pl.BlockSpec((None, LANES, wg), kmap),
                  pl.BlockSpec((None, L, wg), kmap), pl.BlockSpec((None, L, wg), kmap),
                  pl.BlockSpec((None, tq, wg), qmap)],
        out_specs=pl.BlockSpec((None, tq, wg), qmap),
        out_shape=jax.ShapeDtypeStruct((B, T, W_C), BF16),
        compiler_params=_cparams("parallel", "parallel", "arbitrary"),
        name="moba_prompt",
    )(q, kmean, k, v, sg)


def _page_map(layer, n_pages):
    def index_map(b, p, pt):
        return (pt[b, jnp.minimum(p, n_pages - 1)], layer, 0, 0)
    return index_map


def _pad_rows(a, rows):
    return jnp.concatenate([a, jnp.zeros((rows - a.shape[0], a.shape[1]), a.dtype)], axis=0)


def _new_token_bias(nq):
    r = lax.broadcasted_iota(I32, (nq, LANES), 0)
    c = lax.broadcasted_iota(I32, (nq, LANES), 1)
    return jnp.where(c <= r, 0.0, NEG).astype(F32)


def _fold_heads(o, n_heads, nq, width):
    col_head = lax.broadcasted_iota(I32, (nq, n_heads * width), 1) // width
    out = jnp.zeros((nq, n_heads * width), F32)
    for h in range(n_heads):
        out = jnp.where(col_head == h, o[h * nq:(h + 1) * nq, :], out)
    return out


def _flash_ref_update(q, k, v, bias, m_ref, l_ref, acc_ref):
    m, l, acc = _flash_tile(q, k, v, bias, m_ref[...], l_ref[...], acc_ref[...])
    m_ref[...] = m
    l_ref[...] = l
    acc_ref[...] = acc


def _flash_ref_init(m_ref, l_ref, acc_ref):
    m_ref[...] = jnp.full(m_ref.shape, NEG, F32)
    l_ref[...] = jnp.zeros(l_ref.shape, F32)
    acc_ref[...] = jnp.zeros(acc_ref.shape, F32)


def _dsa_sample_select_kernel(pt_ref, qi_ref, wi_ref, kc_ref, kn_ref, bias_ref, key_ref, thr_ref, tie_ref,
                              *, n_pages, nq, topk, col_bits):
    p = pl.program_id(1)
    qi = qi_ref[...]
    wi = wi_ref[...]

    def score(kblk):
        rel = lax.dot_general(qi, kblk, (((1,), (1,)), ((), ())), preferred_element_type=F32)
        wr = wi * jnp.maximum(rel, 0.0)
        s = jnp.zeros((nq, kblk.shape[0]), F32)
        for h in range(H_IDX):
            s = s + wr[h * nq:(h + 1) * nq, :]
        return s

    @pl.when(p < n_pages)
    def _():
        key_ref[p] = _sortable_key(score(kc_ref[...].astype(BF16)))

    @pl.when(p == n_pages)
    def _():
        s = score(_pad_rows(kn_ref[...], LANES).astype(BF16))
        r = lax.broadcasted_iota(I32, (nq, LANES), 0)
        lane = lax.broadcasted_iota(I32, (nq, LANES), 1)
        key_ref[n_pages] = jnp.where(lane <= r, _sortable_key(s), INT_MIN)
        _topk_select(key_ref, thr_ref, tie_ref, n_pages + 1, nq, topk, col_bits, rb=nq)
        thr = thr_ref[...]
        tie = tie_ref[...]

        def bias_chunk(c, carry):
            bias_ref[c] = _select_bias(key_ref[c], thr, tie, c * LANES + lane)
            return carry

        lax.fori_loop(0, n_pages + 1, bias_chunk, 0)


def _dsa_sample_select(page_table, layer, qi_s, wi_s, cache_idx, ki_new):
    Bd, n_pages = page_table.shape
    nq = ki_new.shape[1]
    L = n_pages * cache_idx.shape[2] + nq
    topk = min(TOPK_MAX, L // 4)
    assert cache_idx.shape[2] == LANES and nq == 8
    bmap = lambda b, p, pt: (b, 0, 0)
    return pl.pallas_call(
        functools.partial(_dsa_sample_select_kernel, n_pages=n_pages, nq=nq, topk=topk,
                          col_bits=max(1, (L - 1).bit_length())),
        grid_spec=pltpu.PrefetchScalarGridSpec(
            num_scalar_prefetch=1, grid=(Bd, n_pages + 1),
            in_specs=[pl.BlockSpec((None,) + qi_s.shape[1:], bmap), pl.BlockSpec((None,) + wi_s.shape[1:], bmap),
                      pl.BlockSpec((None, None, LANES, D_IDX), _page_map(layer, n_pages)),
                      pl.BlockSpec((None, nq, D_IDX), bmap)],
            out_specs=pl.BlockSpec((None, n_pages + 1, nq, LANES), lambda b, p, pt: (b, 0, 0, 0)),
            scratch_shapes=[pltpu.VMEM((n_pages + 1, nq, LANES), I32), pltpu.VMEM((nq, LANES), I32),
                            pltpu.VMEM((nq, LANES), I32)]),
        out_shape=jax.ShapeDtypeStruct((Bd, n_pages + 1, nq, LANES), F32),
        compiler_params=_cparams("parallel", "arbitrary"),
        name="dsa_sample_select",
    )(page_table, qi_s, wi_s, cache_idx, ki_new)


def _dsa_sample_attn_kernel(pt_ref, q_ref, kvc_ref, kvn_ref, bias_ref, sg_ref, o_ref, m_ref, l_ref, acc_ref,
                            *, n_pages, nq):
    p = pl.program_id(1)
    q = q_ref[...]
    bias = jnp.concatenate([bias_ref[...]] * H_A, axis=0)

    @pl.when(p == 0)
    def _():
        _flash_ref_init(m_ref, l_ref, acc_ref)

    @pl.when(p < n_pages)
    def _():
        _flash_ref_update(q, kvc_ref[:, 0:W_A].astype(BF16), kvc_ref[:, W_A:2 * W_A].astype(BF16), bias,
                          m_ref, l_ref, acc_ref)

    @pl.when(p == n_pages)
    def _():
        kv = _pad_rows(kvn_ref[...], LANES).astype(BF16)
        _flash_ref_update(q, kv[:, 0:W_A], kv[:, W_A:2 * W_A], bias, m_ref, l_ref, acc_ref)
        o = _fold_heads(acc_ref[...] / l_ref[...], H_A, nq, HEAD_DIM)
        o_ref[...] = (o * sg_ref[...]).astype(BF16)


def _dsa_sample_attn(page_table, layer, q_bd, cache_kv, kv_new, bias, sg):
    Bd, n_pages = page_table.shape
    nq = kv_new.shape[1]
    rows = q_bd.shape[1]
    bmap = lambda b, p, pt: (b, 0, 0)
    return pl.pallas_call(
        functools.partial(_dsa_sample_attn_kernel, n_pages=n_pages, nq=nq),
        grid_spec=pltpu.PrefetchScalarGridSpec(
            num_scalar_prefetch=1, grid=(Bd, n_pages + 1),
            in_specs=[pl.BlockSpec((None, rows, W_A), bmap),
                      pl.BlockSpec((None, None, LANES, 2 * W_A), _page_map(layer, n_pages)),
                      pl.BlockSpec((None, nq, 2 * W_A), bmap),
                      pl.BlockSpec((None, None, nq, LANES), lambda b, p, pt: (b, p, 0, 0)),
                      pl.BlockSpec((None, nq, W_A), bmap)],
            out_specs=pl.BlockSpec((None, nq, W_A), bmap),
            scratch_shapes=[pltpu.VMEM((rows, 1), F32), pltpu.VMEM((rows, 1), F32), pltpu.VMEM((rows, W_A), F32)]),
        out_shape=jax.ShapeDtypeStruct((Bd, nq, W_A), BF16),
        compiler_params=_cparams("parallel", "arbitrary"),
        name="dsa_sample_attn",
    )(page_table, q_bd, cache_kv, kv_new, bias, sg)


def _diff_sample_kernel(pt_ref, q_ref, kvc_ref, kvn_ref, sg_ref, lam_ref, subg_ref, o_ref, m_ref, l_ref, acc_ref,
                        *, n_pages, nq, lam_init):
    p = pl.program_id(1)
    q = q_ref[...]

    @pl.when(p == 0)
    def _():
        _flash_ref_init(m_ref, l_ref, acc_ref)

    @pl.when(p < n_pages)
    def _():
        _flash_ref_update(q, kvc_ref[:, 0:W_B].astype(BF16), kvc_ref[:, W_B:2 * W_B].astype(BF16), None,
                          m_ref, l_ref, acc_ref)

    @pl.when(p == n_pages)
    def _():
        kv = _pad_rows(kvn_ref[...], LANES).astype(BF16)
        bias = jnp.concatenate([_new_token_bias(nq)] * (2 * H_B), axis=0)
        _flash_ref_update(q, kv[:, 0:W_B], kv[:, W_B:2 * W_B], bias, m_ref, l_ref, acc_ref)
        lam_val = _lambda_value(lam_ref, lam_init)
        dv = 2 * DH_B
        l = l_ref[...]
        for h in range(H_B):
            r1, r2 = (2 * h) * nq, (2 * h + 1) * nq
            o = _diff_finish(acc_ref[r1:r1 + nq, h * dv:(h + 1) * dv], l[r1:r1 + nq],
                             acc_ref[r2:r2 + nq, h * dv:(h + 1) * dv], l[r2:r2 + nq],
                             lam_val, subg_ref[...], sg_ref[:, h * dv:(h + 1) * dv], lam_init)
            o_ref[:, h * dv:(h + 1) * dv] = o.astype(BF16)


def _diff_sample(page_table, layer, q_bd, cache_kv, kv_new, sg, lam, subg, lam_init):
    Bd, n_pages = page_table.shape
    nq = kv_new.shape[1]
    rows = q_bd.shape[1]
    bmap = lambda b, p, pt: (b, 0, 0)
    const = lambda b, p, pt: (0, 0)
    return pl.pallas_call(
        functools.partial(_diff_sample_kernel, n_pages=n_pages, nq=nq, lam_init=lam_init),
        grid_spec=pltpu.PrefetchScalarGridSpec(
            num_scalar_prefetch=1, grid=(Bd, n_pages + 1),
            in_specs=[pl.BlockSpec((None, rows, W_B), bmap),
                      pl.BlockSpec((None, None, LANES, 2 * W_B), _page_map(layer, n_pages)),
                      pl.BlockSpec((None, nq, 2 * W_B), bmap),
                      pl.BlockSpec((None, nq, W_B), bmap),
                      pl.BlockSpec(lam.shape, const), pl.BlockSpec(subg.shape, const)],
            out_specs=pl.BlockSpec((None, nq, W_B), bmap),
            scratch_shapes=[pltpu.VMEM((rows, 1), F32), pltpu.VMEM((rows, 1), F32), pltpu.VMEM((rows, W_B), F32)]),
        out_shape=jax.ShapeDtypeStruct((Bd, nq, W_B), BF16),
        compiler_params=_cparams("parallel", "arbitrary"),
        name="diff_sample",
    )(page_table, q_bd, cache_kv, kv_new, sg, lam, subg)


def _moba_sample_gate_kernel(pt_ref, q_ref, kc_ref, sel_ref, ksum_ref, gate_ref, *, n_pages, ppb, n_sel):
    p = pl.program_id(1)
    colsum = jnp.sum(kc_ref[...], axis=0, keepdims=True)

    @pl.when(p % ppb == 0)
    def _():
        ksum_ref[...] = colsum

    @pl.when(p % ppb != 0)
    def _():
        ksum_ref[...] = ksum_ref[...] + colsum

    @pl.when(p == 0)
    def _():
        gate_ref[...] = jnp.zeros(gate_ref.shape, F32)

    @pl.when(p % ppb == ppb - 1)
    def _():
        mean = (ksum_ref[...] * (1.0 / MOBA_BLOCK)).astype(BF16)
        g = lax.dot_general(q_ref[...], jnp.broadcast_to(mean, (8, mean.shape[1])), (((1,), (1,)), ((), ())),
                            preferred_element_type=F32)
        lane = lax.broadcasted_iota(I32, gate_ref.shape, 1)
        gate_ref[...] = jnp.where(lane == p // ppb, g[:, 0:1], gate_ref[...])

    @pl.when(p == n_pages - 1)
    def _():
        sel_ref[...] = _top_blocks(gate_ref[...], n_pages // ppb, n_sel)


def _moba_sample_gate(page_table, layer, q_bd, cache_kv):
    Bd, n_pages = page_table.shape
    rows = q_bd.shape[1]
    ppb = MOBA_BLOCK // LANES
    nbp = n_pages // ppb
    assert n_pages % ppb == 0 and nbp <= LANES
    n_sel = max(1, min(MOBA_TOPB, nbp))
    bmap = lambda b, p, pt: (b, 0, 0)
    return pl.pallas_call(
        functools.partial(_moba_sample_gate_kernel, n_pages=n_pages, ppb=ppb, n_sel=n_sel),
        grid_spec=pltpu.PrefetchScalarGridSpec(
            num_scalar_prefetch=1, grid=(Bd, n_pages),
            in_specs=[pl.BlockSpec((None, rows, W_C), bmap),
                      pl.BlockSpec((None, None, LANES, W_C), _page_map(layer, n_pages))],
            out_specs=pl.BlockSpec((None, rows, LANES), bmap),
            scratch_shapes=[pltpu.VMEM((1, W_C), F32), pltpu.VMEM((rows, LANES), F32)]),
        out_shape=jax.ShapeDtypeStruct((Bd, rows, LANES), F32),
        compiler_params=_cparams("parallel", "arbitrary"),
        name="moba_sample_gate",
    )(page_table, q_bd, cache_kv)


def _moba_sample_attn_kernel(pt_ref, q_ref, kvc_ref, kvn_ref, sel_ref, sg_ref, o_ref, m_ref, l_ref, acc_ref,
                             *, n_pages, nq, ppb):
    p = pl.program_id(1)
    q = q_ref[...]

    @pl.when(p == 0)
    def _():
        _flash_ref_init(m_ref, l_ref, acc_ref)

    @pl.when(p < n_pages)
    def _():
        blk_row = lax.broadcasted_iota(I32, (LANES, LANES), 0)
        onehot = jnp.where(blk_row == p // ppb, 1.0, 0.0).astype(BF16)
        picked = jnp.dot(sel_ref[...].astype(BF16), onehot, preferred_element_type=F32)
        bias = (picked - 1.0) * (-NEG)
        _flash_ref_update(q, kvc_ref[:, 0:W_C].astype(BF16), kvc_ref[:, W_C:2 * W_C].astype(BF16), bias,
                          m_ref, l_ref, acc_ref)

    @pl.when(p == n_pages)
    def _():
        kv = _pad_rows(kvn_ref[...], LANES).astype(BF16)
        bias = jnp.concatenate([_new_token_bias(nq)] * H_C, axis=0)
        _flash_ref_update(q, kv[:, 0:W_C], kv[:, W_C:2 * W_C], bias, m_ref, l_ref, acc_ref)
        o = _fold_heads(acc_ref[...] / l_ref[...], H_C, nq, HEAD_DIM)
        o_ref[...] = (o * sg_ref[...]).astype(BF16)


def _moba_sample_attn(page_table, layer, q_bd, cache_kv, kv_new, sel, sg):
    Bd, n_pages = page_table.shape
    nq = kv_new.shape[1]
    rows = q_bd.shape[1]
    bmap = lambda b, p, pt: (b, 0, 0)
    return pl.pallas_call(
        functools.partial(_moba_sample_attn_kernel, n_pages=n_pages, nq=nq, ppb=MOBA_BLOCK // LANES),
        grid_spec=pltpu.PrefetchScalarGridSpec(
            num_scalar_prefetch=1, grid=(Bd, n_pages + 1),
            in_specs=[pl.BlockSpec((None, rows, W_C), bmap),
                      pl.BlockSpec((None, None, LANES, 2 * W_C), _page_map(layer, n_pages)),
                      pl.BlockSpec((None, nq, 2 * W_C), bmap),
                      pl.BlockSpec((None, rows, LANES), bmap),
                      pl.BlockSpec((None, nq, W_C), bmap)],
            out_specs=pl.BlockSpec((None, nq, W_C), bmap),
            scratch_shapes=[pltpu.VMEM((rows, 1), F32), pltpu.VMEM((rows, 1), F32), pltpu.VMEM((rows, W_C), F32)]),
        out_shape=jax.ShapeDtypeStruct((Bd, nq, W_C), BF16),
        compiler_params=_cparams("parallel", "arbitrary"),
        name="moba_sample_attn",
    )(page_table, q_bd, cache_kv, kv_new, sel, sg)


def _block_diag_queries(q2, Bd, nq, n_heads, width):
    q4 = q2.reshape(Bd, nq, n_heads, width)
    eye = jnp.eye(n_heads, dtype=q2.dtype)
    return jnp.einsum('bqhd,hg->bhqgd', q4, eye).reshape(Bd, n_heads * nq, n_heads * width)


def _sample_trunk(x, past_len, caches, page_table, norm_g, w_even, w_out_even, lam_even, subln_g_even,
                  w_odd, w_out_odd, final_norm_g):
    cache_a_kv, cache_a_idx, cache_b_kv, cache_c_kv = caches
    Bd, nq, D = x.shape
    M = Bd * nq
    n_pool, _, page, _ = cache_a_idx.shape
    assert page == LANES and past_len % MOBA_BLOCK == 0 and nq <= 8
    ca_kv = cache_a_kv.reshape(n_pool, cache_a_kv.shape[1], page, 2 * W_A)
    cb_kv = cache_b_kv.reshape(n_pool, cache_b_kv.shape[1], page, 2 * W_B)
    cc_kv = cache_c_kv.reshape(n_pool, cache_c_kv.shape[1], page, 2 * W_C)
    depth = norm_g.shape[0]
    cos, sin = _rope_tables(past_len + jnp.arange(nq, dtype=I32))
    cos, sin = jnp.tile(cos, (Bd, 1)), jnp.tile(sin, (Bd, 1))
    x2 = x.reshape(M, D)
    a_kv, a_idx, b_kv, c_kv = [], [], [], []
    y = None
    r3 = lambda a: a.reshape(Bd, nq, a.shape[1])
    for li in range(depth):
        j = li // 2
        g = norm_g[li][None, :]
        last = li == depth - 1
        gf = final_norm_g[None, :] if last else None
        if li % 2 == 0:
            lam_init = 0.8 - 0.6 * math.exp(-0.3 * li)
            (qa, kva, _, _, sg, qi, kif, _, wi, qb, kvb, _, _) = _even_proj(x2, g, *w_even[j], cos, sin, M)
            qi_s = qi.reshape(Bd, nq, H_IDX, D_IDX).transpose(0, 2, 1, 3).reshape(Bd, H_IDX * nq, D_IDX)
            wi_s = wi.reshape(Bd, nq, H_IDX).transpose(0, 2, 1).reshape(Bd, H_IDX * nq, 1)
            bias = _dsa_sample_select(page_table, j, qi_s, wi_s, cache_a_idx, r3(kif))
            oa = _dsa_sample_attn(page_table, j, _block_diag_queries(qa, Bd, nq, H_A, HEAD_DIM), ca_kv,
                                  r3(kva), bias, r3(sg[:, :W_A]))
            ob = _diff_sample(page_table, j, _block_diag_queries(qb, Bd, nq, 2 * H_B, DH_B), cb_kv, r3(kvb),
                              r3(sg[:, W_A:]), lam_even[j], subln_g_even[j][None, :], lam_init)
            res = _out_proj([oa.reshape(M, W_A), ob.reshape(M, W_B)], w_out_even[j], x2, gf, M)
            a_kv.append(kva.reshape(Bd, nq, 2, H_A, HEAD_DIM))
            a_idx.append(kif.reshape(Bd, nq, D_IDX))
            b_kv.append(kvb.reshape(Bd, nq, 2, H_B, 2 * DH_B))
        else:
            q, kv, _, _, sg, _ = _odd_proj(x2, g, w_odd[j], cos, sin, M)
            q_bd = _block_diag_queries(q, Bd, nq, H_C, HEAD_DIM)
            sel = _moba_sample_gate(page_table, j, q_bd, cc_kv)
            o = _moba_sample_attn(page_table, j, q_bd, cc_kv, r3(kv), sel, r3(sg))
            res = _out_proj([o.reshape(M, W_C)], w_out_odd[j], x2, gf, M)
            c_kv.append(kv.reshape(Bd, nq, 2, H_C, HEAD_DIM))
        x2 = res[0]
        if last:
            y = res[1]
    return (y.reshape(Bd, nq, D), jnp.stack(a_kv, axis=1), jnp.stack(a_idx, axis=1),
            jnp.stack(b_kv, axis=1), jnp.stack(c_kv, axis=1))


def _rope_tables(pos):
    half = HEAD_DIM // 2
    inv_freq = jnp.exp(-math.log(ROPE_THETA) * jnp.arange(half, dtype=F32) / half)
    ang = pos.astype(F32)[:, None] * inv_freq[None, :]
    cos, sin = jnp.cos(ang), jnp.sin(ang)
    return jnp.concatenate([cos, cos, cos, cos], axis=1), jnp.concatenate([-sin, sin, -sin, sin], axis=1)


def _split_even_weight(w):
    wa = w[:, 0:4 * W_A].astype(BF16)
    wi = jnp.pad(w[:, 4 * W_A:4 * W_A + IDX_COLS], ((0, 0), (0, IDX_PAD - IDX_COLS))).astype(BF16)
    wb = w[:, 4 * W_A + IDX_COLS:].astype(BF16)
    return wa, wi, wb


def _prompt_trunk(x, norm_g, w_even, w_out_even, lam_even, subln_g_even, w_odd, w_out_odd, final_norm_g):
    B, T, D = x.shape
    M = B * T
    tm = min(256, T)
    depth = norm_g.shape[0]
    cos, sin = _rope_tables(jnp.arange(T, dtype=I32))
    cos, sin = jnp.tile(cos, (B, 1)), jnp.tile(sin, (B, 1))
    x2 = x.reshape(M, D)
    a_kv, a_idx, b_kv, c_kv = [], [], [], []
    y = None
    for li in range(depth):
        j = li // 2
        g = norm_g[li][None, :]
        last = li == depth - 1
        gf = final_norm_g[None, :] if last else None
        if li % 2 == 0:
            lam_init = 0.8 - 0.6 * math.exp(-0.3 * li)
            (qa, kva, ka, va, sg, qi, kif, kib, wi, qb, kvb, kb, vb) = _even_proj(
                x2, g, *w_even[j], cos, sin, tm)
            r3 = lambda a: a.reshape(B, T, a.shape[1])
            oa = _dsa_prompt(r3(qa), r3(qi), r3(wi), r3(kib), r3(ka), r3(va), r3(sg[:, :W_A]), tm)
            ob = _diff_prompt(r3(qb), r3(kb), r3(vb), r3(sg[:, W_A:]), lam_even[j],
                              subln_g_even[j][None, :], lam_init, tm)
            res = _out_proj([oa.reshape(M, W_A), ob.reshape(M, W_B)], w_out_even[j], x2, gf, tm)
            a_kv.append(kva.reshape(B, T, 2, H_A, HEAD_DIM))
            a_idx.append(kif.reshape(B, T, D_IDX))
            b_kv.append(kvb.reshape(B, T, 2, H_B, 2 * DH_B))
        else:
            assert T % MOBA_BLOCK == 0
            q, kv, k, v, sg, ksum = _odd_proj(x2, g, w_odd[j], cos, sin, MOBA_BLOCK)
            nb = T // MOBA_BLOCK
            kmean = ksum.reshape(B, nb, W_C) * (1.0 / MOBA_BLOCK)
            kmean = jnp.pad(kmean, ((0, 0), (0, LANES - nb), (0, 0)))
            r3 = lambda a: a.reshape(B, T, a.shape[1])
            o = _moba_prompt(r3(q), kmean, r3(k), r3(v), r3(sg))
            res = _out_proj([o.reshape(M, W_C)], w_out_odd[j], x2, gf, tm)
            c_kv.append(kv.reshape(B, T, 2, H_C, HEAD_DIM))
        x2 = res[0]
        if last:
            y = res[1]
    return (y.reshape(B, T, D), jnp.stack(a_kv, axis=1), jnp.stack(a_idx, axis=1),
            jnp.stack(b_kv, axis=1), jnp.stack(c_kv, axis=1))


def kernel(x_prompt, x_sample, cache_a_kv, cache_a_idx, cache_b_kv, cache_c_kv, page_table, norm_g,
           w_in_even, w_out_even, lam_even, subln_g_even, w_in_odd, w_out_odd, final_norm_g):
    w_even = [_split_even_weight(w_in_even[j]) for j in range(w_in_even.shape[0])]
    w_odd = [w_in_odd[j].astype(BF16) for j in range(w_in_odd.shape[0])]
    w_out_e = [w_out_even[j].astype(BF16) for j in range(w_out_even.shape[0])]
    w_out_o = [w_out_odd[j].astype(BF16) for j in range(w_out_odd.shape[0])]
    weights = (norm_g, w_even, w_out_e, lam_even, subln_g_even, w_odd, w_out_o, final_norm_g)
    y_p, a_kv_p, a_idx_p, b_kv_p, c_kv_p = _prompt_trunk(x_prompt, *weights)
    past_len = page_table.shape[1] * cache_a_idx.shape[2]
    y_s, a_kv_s, a_idx_s, b_kv_s, c_kv_s = _sample_trunk(
        x_sample, past_len, (cache_a_kv, cache_a_idx, cache_b_kv, cache_c_kv), page_table, *weights)
    return (y_p, y_s, a_kv_p, a_idx_p, b_kv_p, c_kv_p, a_kv_s, a_idx_s, b_kv_s, c_kv_s)
```

```python
import functools
import math

import jax
import jax.numpy as jnp
from jax import lax
from jax.experimental import pallas as pl
from jax.experimental.pallas import tpu as pltpu

F32 = jnp.float32
BF16 = jnp.bfloat16
I32 = jnp.int32

HEAD_DIM = 64
H_A = 8
H_IDX = 8
D_IDX = 64
TOPK_MAX = 256
H_B = 4
DH_B = 64
H_C = 16
MOBA_BLOCK = 256
MOBA_TOPB = 3
ROPE_THETA = 10000.0
NORM_EPS = 1e-6
SUBLN_EPS = 1e-5
W_A = H_A * HEAD_DIM
W_B = H_B * 2 * DH_B
W_C = H_C * HEAD_DIM
IDX_COLS = H_IDX * D_IDX + D_IDX + H_IDX
IDX_PAD = 640
LANES = 128
NEG = -1e30
INT_MIN = -2 ** 31
VMEM_LIMIT = 56 * 1024 * 1024


def _cparams(*sem):
    return pltpu.CompilerParams(dimension_semantics=sem, vmem_limit_bytes=VMEM_LIMIT)


def _resident(block_shape, index_map):
    return pl.BlockSpec(block_shape, index_map, pipeline_mode=pl.Buffered(1))


def _rms(x, g, eps):
    ms = jnp.mean(x * x, axis=-1, keepdims=True)
    return x * lax.rsqrt(ms + eps) * g


def _silu(g):
    return g / (1.0 + jnp.exp(-g))


def _rope128(xc, cos, sin, first_half):
    sw = jnp.where(first_half, pltpu.roll(xc, 96, 1), pltpu.roll(xc, 32, 1))
    return xc * cos + sw * sin


def _rope_wide(y, cos, sin, first_half):
    return jnp.concatenate(
        [_rope128(y[:, c * LANES:(c + 1) * LANES], cos, sin, first_half) for c in range(y.shape[1] // LANES)],
        axis=1)


def _first_half_mask(rows):
    lane = lax.broadcasted_iota(I32, (rows, LANES), 1)
    return (lane % HEAD_DIM) < (HEAD_DIM // 2)


def _flash_tile(q, k, v, bias, m, l, acc):
    s = lax.dot_general(q, k, (((1,), (1,)), ((), ())), preferred_element_type=F32)
    if bias is not None:
        s = s + bias
    m_new = jnp.maximum(m, jnp.max(s, axis=1, keepdims=True))
    alpha = jnp.exp(m - m_new)
    p = jnp.exp(s - m_new)
    l = alpha * l + jnp.sum(p, axis=1, keepdims=True)
    acc = alpha * acc + jnp.dot(p.astype(BF16), v, preferred_element_type=F32)
    return m_new, l, acc


def _causal_bias(rows, cols):
    r = lax.broadcasted_iota(I32, (rows, cols), 0)
    c = lax.broadcasted_iota(I32, (rows, cols), 1)
    return jnp.where(c <= r, 0.0, NEG).astype(F32)


def _sortable_key(score):
    bits = lax.bitcast_convert_type(score, I32)
    return jnp.where(bits < 0, bits ^ jnp.int32(0x7FFFFFFF), bits)


def _even_proj_kernel(x_ref, g_ref, wa_ref, wi_ref, wb_ref, cos_ref, sin_ref,
                      qa_ref, kva_ref, ka_ref, va_ref, sg_ref, qi_ref, kif_ref, kib_ref, wio_ref,
                      qb_ref, kvb_ref, kb_ref, vb_ref):
    h = _rms(x_ref[...], g_ref[...], NORM_EPS).astype(BF16)
    cos = cos_ref[...]
    sin = sin_ref[...]
    fh = _first_half_mask(h.shape[0])
    scale = HEAD_DIM ** -0.5

    def proj(w_ref, lo, hi):
        return jnp.dot(h, w_ref[:, lo:hi], preferred_element_type=F32)

    for w_ref, q_ref, kv_ref, k_ref, v_ref, g_lo in ((wa_ref, qa_ref, kva_ref, ka_ref, va_ref, 0),
                                                      (wb_ref, qb_ref, kvb_ref, kb_ref, vb_ref, W_A)):
        q = _rope_wide(proj(w_ref, 0, 512), cos, sin, fh)
        q_ref[...] = (q * scale).astype(BF16)
        k = _rope_wide(proj(w_ref, 512, 1024), cos, sin, fh)
        v = proj(w_ref, 1024, 1536)
        kv_ref[:, 0:512] = k
        kv_ref[:, 512:1024] = v
        k_ref[...] = k.astype(BF16)
        v_ref[...] = v.astype(BF16)
        sg_ref[:, g_lo:g_lo + 512] = _silu(proj(w_ref, 1536, 2048))

    qi_ref[...] = _rope_wide(proj(wi_ref, 0, 512), cos, sin, fh).astype(BF16)
    tail = proj(wi_ref, 512, IDX_PAD)
    ki = _rope128(tail, cos, sin, fh)[:, 0:D_IDX]
    kif_ref[...] = ki
    kib_ref[...] = ki.astype(BF16)
    wio_ref[...] = tail[:, D_IDX:D_IDX + H_IDX]


def _even_proj(x2, g, wa, wi, wb, cos, sin, tm):
    M, D = x2.shape
    row = lambda i: (i, 0)
    const = lambda i: (0, 0)
    out_cols = ((512, BF16), (1024, F32), (512, BF16), (512, BF16), (1024, F32), (512, BF16), (D_IDX, F32),
                (D_IDX, BF16), (H_IDX, F32), (512, BF16), (1024, F32), (512, BF16), (512, BF16))
    return pl.pallas_call(
        _even_proj_kernel,
        grid=(M // tm,),
        in_specs=[pl.BlockSpec((tm, D), row), pl.BlockSpec((1, D), const),
                  _resident(wa.shape, const), _resident(wi.shape, const), _resident(wb.shape, const),
                  pl.BlockSpec((tm, LANES), row), pl.BlockSpec((tm, LANES), row)],
        out_specs=[pl.BlockSpec((tm, c), row) for c, _ in out_cols],
        out_shape=[jax.ShapeDtypeStruct((M, c), dt) for c, dt in out_cols],
        compiler_params=_cparams("parallel"),
        name="even_proj",
    )(x2, g, wa, wi, wb, cos, sin)


def _odd_proj_kernel(x_ref, g_ref, w_ref, cos_ref, sin_ref, q_ref, kv_ref, k_ref, v_ref, sg_ref, ksum_ref):
    h = _rms(x_ref[...], g_ref[...], NORM_EPS).astype(BF16)
    cos = cos_ref[...]
    sin = sin_ref[...]
    fh = _first_half_mask(h.shape[0])
    scale = HEAD_DIM ** -0.5
    for c in range(W_C // 512):
        lo = c * 512
        q = _rope_wide(jnp.dot(h, w_ref[:, lo:lo + 512], preferred_element_type=F32), cos, sin, fh)
        q_ref[:, lo:lo + 512] = (q * scale).astype(BF16)
        k = _rope_wide(jnp.dot(h, w_ref[:, W_C + lo:W_C + lo + 512], preferred_element_type=F32), cos, sin, fh)
        kv_ref[:, lo:lo + 512] = k
        k_ref[:, lo:lo + 512] = k.astype(BF16)
        ksum_ref[:, lo:lo + 512] = jnp.sum(k, axis=0, keepdims=True)
        v = jnp.dot(h, w_ref[:, 2 * W_C + lo:2 * W_C + lo + 512], preferred_element_type=F32)
        kv_ref[:, W_C + lo:W_C + lo + 512] = v
        v_ref[:, lo:lo + 512] = v.astype(BF16)
        sg_ref[:, lo:lo + 512] = _silu(jnp.dot(h, w_ref[:, 3 * W_C + lo:3 * W_C + lo + 512],
                                               preferred_element_type=F32))


def _odd_proj(x2, g, w, cos, sin, tm):
    M, D = x2.shape
    row = lambda i: (i, 0)
    const = lambda i: (0, 0)
    out_cols = ((W_C, BF16), (2 * W_C, F32), (W_C, BF16), (W_C, BF16), (W_C, F32))
    return pl.pallas_call(
        _odd_proj_kernel,
        grid=(M // tm,),
        in_specs=[pl.BlockSpec((tm, D), row), pl.BlockSpec((1, D), const), _resident(w.shape, const),
                  pl.BlockSpec((tm, LANES), row), pl.BlockSpec((tm, LANES), row)],
        out_specs=[pl.BlockSpec((tm, c), row) for c, _ in out_cols]
                  + [pl.BlockSpec((None, 1, W_C), lambda i: (i, 0, 0))],
        out_shape=[jax.ShapeDtypeStruct((M, c), dt) for c, dt in out_cols]
                  + [jax.ShapeDtypeStruct((M // tm, 1, W_C), F32)],
        compiler_params=_cparams("parallel"),
        name="odd_proj",
    )(x2, g, w, cos, sin)


def _out_proj_kernel(*refs, n_in, final):
    m_refs = refs[:n_in]
    w_ref, x_ref = refs[n_in], refs[n_in + 1]
    acc = x_ref[...]
    lo = 0
    for m_ref in m_refs:
        kk = m_ref.shape[1]
        acc = acc + jnp.dot(m_ref[...], w_ref[lo:lo + kk, :], preferred_element_type=F32)
        lo += kk
    if final:
        gf_ref, o_ref, y_ref = refs[n_in + 2:]
        y_ref[...] = _rms(acc, gf_ref[...], NORM_EPS)
    else:
        o_ref = refs[n_in + 2]
    o_ref[...] = acc


def _out_proj(mixed, w, x2, gf, tm):
    M, D = x2.shape
    row = lambda i: (i, 0)
    const = lambda i: (0, 0)
    final = gf is not None
    in_specs = [pl.BlockSpec((tm, m.shape[1]), row) for m in mixed]
    in_specs += [_resident(w.shape, const), pl.BlockSpec((tm, D), row)]
    args = list(mixed) + [w, x2]
    out_specs = [pl.BlockSpec((tm, D), row)]
    out_shape = [jax.ShapeDtypeStruct((M, D), F32)]
    if final:
        in_specs.append(pl.BlockSpec((1, D), const))
        args.append(gf)
        out_specs.append(pl.BlockSpec((tm, D), row))
        out_shape.append(jax.ShapeDtypeStruct((M, D), F32))
    return pl.pallas_call(
        functools.partial(_out_proj_kernel, n_in=len(mixed), final=final),
        grid=(M // tm,), in_specs=in_specs, out_specs=out_specs, out_shape=out_shape,
        compiler_params=_cparams("parallel"),
        name="out_proj",
    )(*args)


def _topk_select(key_ref, thr_ref, tie_ref, nch, rows, topk, col_bits, rb):
    for r0 in range(0, rows, rb):
        def count(pred):
            def body(c, acc):
                return acc + pred(key_ref[c, r0:r0 + rb, :], c).astype(I32)
            acc = lax.fori_loop(0, nch, body, jnp.zeros((rb, LANES), I32))
            return jnp.sum(acc, axis=1, keepdims=True)

        def count_ge(cand):
            cb = jnp.broadcast_to(cand, (rb, LANES))
            return count(lambda blk, c: blk >= cb)

        t0 = jnp.full((rb, 1), INT_MIN, I32)
        t0 = jnp.where(count_ge(jnp.zeros((rb, 1), I32)) >= topk, 0, t0)

        def bit_step(i, t):
            cand = t | jnp.left_shift(jnp.int32(1), 30 - i)
            return jnp.where(count_ge(cand) >= topk, cand, t)

        t = lax.fori_loop(0, 31, bit_step, t0)
        tb = jnp.broadcast_to(t, (rb, LANES))
        n_gt = count(lambda blk, c: blk > tb)
        n_ge = count(lambda blk, c: blk >= tb)
        need = topk - n_gt
        thr_ref[r0:r0 + rb, :] = tb
        tie_ref[r0:r0 + rb, :] = jnp.full((rb, LANES), 2 ** 30, I32)

        @pl.when(jnp.max(n_ge) > topk)
        def _():
            lane = lax.broadcasted_iota(I32, (rb, LANES), 1)

            def tie_step(i, vmax):
                cand = vmax | jnp.left_shift(jnp.int32(1), col_bits - 1 - i)
                cb = jnp.broadcast_to(cand, (rb, LANES))
                below = count(lambda blk, c: (blk == tb) & (c * LANES + lane < cb))
                return jnp.where(below < need, cand, vmax)

            tie = lax.fori_loop(0, col_bits, tie_step, jnp.zeros((rb, 1), I32))
            tie = jnp.where(n_ge > topk, tie, 2 ** 30)
            tie_ref[r0:r0 + rb, :] = jnp.broadcast_to(tie, (rb, LANES))


def _select_bias(key, thr, tie, col):
    sel = (key > thr) | ((key == thr) & (col <= tie) & (key != INT_MIN))
    return jnp.where(sel, 0.0, NEG).astype(F32)


def _block_diag_qt(qt_pair, tq):
    z = jnp.zeros((HEAD_DIM, tq), qt_pair.dtype)
    return jnp.concatenate([jnp.concatenate([qt_pair[0:HEAD_DIM], z], axis=1),
                            jnp.concatenate([z, qt_pair[HEAD_DIM:2 * HEAD_DIM]], axis=1)], axis=0)


def _flash_tile_t(s, vts, m, l, acc):
    tq = s.shape[1] // 2
    m_new = jnp.maximum(m, jnp.max(s, axis=0, keepdims=True))
    alpha = jnp.exp(m - m_new)
    p = jnp.exp(s - m_new)
    l = alpha * l + jnp.sum(p, axis=0, keepdims=True)
    pb = p.astype(BF16)
    if len(vts) == 1:
        pv = jnp.dot(vts[0], pb, preferred_element_type=F32)
    else:
        pv = jnp.concatenate([jnp.dot(vts[0], pb[:, :tq], preferred_element_type=F32),
                              jnp.dot(vts[1], pb[:, tq:], preferred_element_type=F32)], axis=1)
    return m_new, l, alpha * acc + pv


def _flash_init_t(dv, tq):
    return (jnp.full((1, 2 * tq), NEG, F32), jnp.zeros((1, 2 * tq), F32), jnp.zeros((dv, 2 * tq), F32))


def _causal_bias_t(tq):
    key = lax.broadcasted_iota(I32, (tq, tq), 0)
    qry = lax.broadcasted_iota(I32, (tq, tq), 1)
    b = jnp.where(key <= qry, 0.0, NEG).astype(F32)
    return jnp.concatenate([b, b], axis=1)


def _pair_to_rows(o, tq):
    return jnp.concatenate([o[:, :tq], o[:, tq:]], axis=0).T


def _topk_select_t(key_ref, n_rows, tq, topk, col_bits):
    rb = tq
    n_acc = 4
    nblk = n_rows // rb
    rowi = lax.broadcasted_iota(I32, (8, tq), 0)

    def count(pred):
        def body(c, accs):
            r0 = pl.multiple_of(c * rb, rb)
            accs = list(accs)
            blk = key_ref[pl.ds(r0, rb), :]
            for g in range(rb // 8):
                hit = pred(blk[8 * g:8 * g + 8], r0 + 8 * g)
                accs[g % n_acc] = accs[g % n_acc] + hit.astype(I32)
            return tuple(accs)
        accs = lax.fori_loop(0, nblk, body, tuple(jnp.zeros((8, tq), I32) for _ in range(n_acc)))
        return jnp.sum(sum(accs[1:], accs[0]), axis=0, keepdims=True)

    def count_ge(cand):
        cb = jnp.broadcast_to(cand, (8, tq))
        return count(lambda blk, r0: blk >= cb)

    t0 = jnp.where(count_ge(jnp.zeros((1, tq), I32)) >= topk, 0, INT_MIN).astype(I32)

    def bit_step(b, t):
        cand = t | jnp.left_shift(jnp.int32(1), 30 - b)
        return jnp.where(count_ge(cand) >= topk, cand, t)

    thr = lax.fori_loop(0, 31, bit_step, t0)
    tb = jnp.broadcast_to(thr, (8, tq))
    n_gt = count(lambda blk, r0: blk > tb)
    n_ge = count(lambda blk, r0: blk >= tb)
    need = topk - n_gt

    def tie_search():
        def tie_step(b, vmax):
            cand = vmax | jnp.left_shift(jnp.int32(1), col_bits - 1 - b)
            cb = jnp.broadcast_to(cand, (8, tq))
            below = count(lambda blk, r0: (blk == tb) & (r0 + rowi < cb))
            return jnp.where(below < need, cand, vmax)
        tie = lax.fori_loop(0, col_bits, tie_step, jnp.zeros((1, tq), I32))
        return jnp.where(n_ge > topk, tie, 2 ** 30)

    tie = lax.cond(jnp.max(n_ge) > topk, tie_search, lambda: jnp.full((1, tq), 2 ** 30, I32))
    return thr, tie


def _dsa_prompt_kernel(qat_ref, qit_ref, wit_ref, ki_ref, ka_ref, vat_ref, sg_ref, o_ref, key_ref,
                       *, tq, topk, col_bits):
    i = pl.program_id(1)
    nkt = i + 1
    n_units = W_A // LANES
    keyi = lax.broadcasted_iota(I32, (tq, tq), 0)
    qryi = lax.broadcasted_iota(I32, (tq, tq), 1)
    wit = wit_ref[...]

    def score_tile(kt, carry):
        start = pl.multiple_of(kt * tq, tq)
        ki = ki_ref[pl.ds(start, tq), :]
        score = jnp.zeros((tq, tq), F32)
        for h in range(H_IDX):
            rel = jnp.dot(ki, qit_ref[h * D_IDX:(h + 1) * D_IDX, :], preferred_element_type=F32)
            score = score + wit[h:h + 1, :] * jnp.maximum(rel, 0.0)
        key = _sortable_key(score)
        key_ref[pl.ds(start, tq), :] = jnp.where((kt < i) | (keyi <= qryi), key, INT_MIN)
        return carry

    lax.fori_loop(0, nkt, score_tile, 0)
    thr, tie = _topk_select_t(key_ref, nkt * tq, tq, topk, col_bits)
    qbd = [_block_diag_qt(qat_ref[u * LANES:(u + 1) * LANES, :], tq) for u in range(n_units)]

    def kv_tile(kt, carry):
        start = pl.multiple_of(kt * tq, tq)
        key = key_ref[pl.ds(start, tq), :]
        sel = (key > thr) | ((key == thr) & (kt * tq + keyi <= tie) & (key != INT_MIN))
        bias = jnp.where(sel, 0.0, NEG).astype(F32)
        bias = jnp.concatenate([bias, bias], axis=1)
        out = []
        for u in range(n_units):
            s = jnp.dot(ka_ref[pl.ds(start, tq), u * LANES:(u + 1) * LANES], qbd[u],
                        preferred_element_type=F32) + bias
            vts = [vat_ref[kt, u * LANES:u * LANES + HEAD_DIM, :],
                   vat_ref[kt, u * LANES + HEAD_DIM:(u + 1) * LANES, :]]
            out.append(_flash_tile_t(s, vts, *carry[u]))
        return tuple(out)

    carry = lax.fori_loop(0, nkt, kv_tile, tuple(_flash_init_t(HEAD_DIM, tq) for _ in range(n_units)))
    for u in range(n_units):
        _, l, acc = carry[u]
        o = _pair_to_rows(acc / l, tq) * sg_ref[:, u * LANES:(u + 1) * LANES]
        o_ref[:, u * LANES:(u + 1) * LANES] = o.astype(BF16)


def _dsa_prompt(qat, qit, wit, ki, ka, vat, sg, tq):
    B, _, T = qat.shape
    L = ka.shape[1]
    topk = min(TOPK_MAX, L // 4)
    assert T == L and T % tq == 0 and tq % LANES == 0
    qtmap = lambda b, i: (b, 0, i)
    kmap = lambda b, i: (b, 0, 0)
    return pl.pallas_call(
        functools.partial(_dsa_prompt_kernel, tq=tq, topk=topk, col_bits=max(1, (L - 1).bit_length())),
        grid=(B, T // tq),
        in_specs=[pl.BlockSpec((None, W_A, tq), qtmap), pl.BlockSpec((None, H_IDX * D_IDX, tq), qtmap),
                  pl.BlockSpec((None, H_IDX, tq), qtmap),
                  _resident((None, L, D_IDX), kmap), _resident((None, L, W_A), kmap),
                  _resident((None, L // tq, W_A, tq), lambda b, i: (b, 0, 0, 0)),
                  pl.BlockSpec((None, tq, W_A), lambda b, i: (b, i, 0))],
        out_specs=pl.BlockSpec((None, tq, W_A), lambda b, i: (b, i, 0)),
        out_shape=jax.ShapeDtypeStruct((B, T, W_A), BF16),
        scratch_shapes=[pltpu.VMEM((L, tq), I32)],
        compiler_params=_cparams("parallel", "arbitrary"),
        name="dsa_prompt",
    )(qat, qit, wit, ki, ka, vat, sg)


def _lambda_value(lam_ref, lam_init):
    lam = lam_ref[...]
    s1 = jnp.sum(lam[0:1] * lam[1:2], axis=1, keepdims=True)
    s2 = jnp.sum(lam[2:3] * lam[3:4], axis=1, keepdims=True)
    return jnp.exp(s1) - jnp.exp(s2) + lam_init


def _diff_finish(acc1, l1, acc2, l2, lam_val, subg, sg, lam_init):
    o = acc1 / l1 - lam_val * (acc2 / l2)
    return _rms(o, subg, SUBLN_EPS) * (1.0 - lam_init) * sg


def _diff_prompt_kernel(qt_ref, k_ref, vt_ref, sg_ref, lam_ref, subg_ref, o_ref, *, tq, lam_init):
    i = pl.program_id(1)
    lam_val = _lambda_value(lam_ref, lam_init)
    diag = _causal_bias_t(tq)
    dv = 2 * DH_B
    qbd = [_block_diag_qt(qt_ref[h * dv:(h + 1) * dv, :], tq) for h in range(H_B)]

    def tile(kt, carry, bias):
        start = pl.multiple_of(kt * tq, tq)
        out = []
        for h in range(H_B):
            s = jnp.dot(k_ref[pl.ds(start, tq), h * dv:(h + 1) * dv], qbd[h], preferred_element_type=F32)
            if bias is not None:
                s = s + bias
            out.append(_flash_tile_t(s, [vt_ref[kt, h * dv:(h + 1) * dv, :]], *carry[h]))
        return tuple(out)

    carry = lax.fori_loop(0, i, lambda kt, c: tile(kt, c, None), tuple(_flash_init_t(dv, tq) for _ in range(H_B)))
    carry = tile(i, carry, diag)
    for h in range(H_B):
        _, l, acc = carry[h]
        o = acc / l
        o = (o[:, :tq] - lam_val * o[:, tq:]).T
        o = _rms(o, subg_ref[...], SUBLN_EPS) * (1.0 - lam_init) * sg_ref[:, h * dv:(h + 1) * dv]
        o_ref[:, h * dv:(h + 1) * dv] = o.astype(BF16)


def _diff_prompt(qbt, kb, vbt, sg, lam, subg, lam_init, tq):
    B, _, T = qbt.shape
    L = kb.shape[1]
    assert T == L and T % tq == 0
    qmap = lambda b, i: (b, i, 0)
    kmap = lambda b, i: (b, 0, 0)
    const = lambda b, i: (0, 0)
    return pl.pallas_call(
        functools.partial(_diff_prompt_kernel, tq=tq, lam_init=lam_init),
        grid=(B, T // tq),
        in_specs=[pl.BlockSpec((None, W_B, tq), lambda b, i: (b, 0, i)), _resident((None, L, W_B), kmap),
                  _resident((None, L // tq, W_B, tq), lambda b, i: (b, 0, 0, 0)),
                  pl.BlockSpec((None, tq, W_B), qmap),
                  pl.BlockSpec(lam.shape, const), pl.BlockSpec(subg.shape, const)],
        out_specs=pl.BlockSpec((None, tq, W_B), qmap),
        out_shape=jax.ShapeDtypeStruct((B, T, W_B), BF16),
        compiler_params=_cparams("parallel", "arbitrary"),
        name="diff_prompt",
    )(qbt, kb, vbt, sg, lam, subg)


def _top_blocks(gate, n_valid, n_sel):
    lane = lax.broadcasted_iota(I32, gate.shape, 1)
    g = jnp.where(lane < n_valid, gate, -jnp.inf)
    sel = jnp.zeros(gate.shape, F32)
    for _ in range(n_sel):
        mx = jnp.max(g, axis=1, keepdims=True)
        first = jnp.min(jnp.where(g == mx, lane, LANES), axis=1, keepdims=True)
        pick = (lane == first) & (lane < n_valid)
        sel = jnp.where(pick, 1.0, sel)
        g = jnp.where(lane == first, -jnp.inf, g)
    return sel


def _top_blocks_t(gate, n_valid, n_sel):
    blk = lax.broadcasted_iota(I32, gate.shape, 0)
    g = jnp.where(blk < n_valid, gate, -jnp.inf)
    sel = jnp.zeros(gate.shape, F32)
    for _ in range(n_sel):
        mx = jnp.max(g, axis=0, keepdims=True)
        first = jnp.min(jnp.where(g == mx, blk, LANES), axis=0, keepdims=True)
        pick = (blk == first) & (blk < n_valid)
        sel = jnp.where(pick, 1.0, sel)
        g = jnp.where(blk == first, -jnp.inf, g)
    return sel


def _moba_prompt_kernel(qt_ref, kmean_ref, k_ref, vt_ref, sg_ref, o_ref, bias_ref, *, tq, n_sel, n_units):
    i = pl.program_id(2)
    diag = _causal_bias_t(tq)
    qbd = []
    for u in range(n_units):
        qbd.append(_block_diag_qt(qt_ref[u * LANES:(u + 1) * LANES, :], tq))
        gate = jnp.dot(kmean_ref[:, u * LANES:(u + 1) * LANES].astype(BF16), qbd[u],
                       preferred_element_type=F32)
        bias_ref[u] = (_top_blocks_t(gate, i, n_sel) - 1.0) * (-NEG)

    def tile(kt, carry, diagonal):
        start = pl.multiple_of(kt * tq, tq)
        out = []
        for u in range(n_units):
            bias = diag if diagonal else bias_ref[u, pl.ds(kt, 1), :]
            s = jnp.dot(k_ref[pl.ds(start, tq), u * LANES:(u + 1) * LANES], qbd[u],
                        preferred_element_type=F32) + bias
            vts = [vt_ref[kt, u * LANES:u * LANES + HEAD_DIM, :],
                   vt_ref[kt, u * LANES + HEAD_DIM:(u + 1) * LANES, :]]
            out.append(_flash_tile_t(s, vts, *carry[u]))
        return tuple(out)

    carry = lax.fori_loop(0, i, lambda kt, c: tile(kt, c, False),
                          tuple(_flash_init_t(HEAD_DIM, tq) for _ in range(n_units)))
    carry = tile(i, carry, True)
    for u in range(n_units):
        _, l, acc = carry[u]
        o = _pair_to_rows(acc / l, tq) * sg_ref[:, u * LANES:(u + 1) * LANES]
        o_ref[:, u * LANES:(u + 1) * LANES] = o.astype(BF16)


def _moba_prompt(qt, kmean, k, vt, sg, hg=8):
    B, _, T = qt.shape
    L = k.shape[1]
    tq = MOBA_BLOCK
    nb = L // tq
    assert T == L and L % tq == 0 and nb <= LANES
    n_sel = max(1, min(MOBA_TOPB, nb - 1))
    wg = hg * HEAD_DIM
    n_units = wg // LANES
    qmap = lambda b, g, i: (b, i, g)
    kmap = lambda b, g, i: (b, 0, g)
    return pl.pallas_call(
        functools.partial(_moba_prompt_kernel, tq=tq, n_sel=n_sel, n_units=n_units),
        grid=(B, W_C // wg, T // tq),
        in_specs=[pl.BlockSpec((None, wg, tq), lambda b, g, i: (b, g, i)),
                  _resident((None, LANES, wg), kmap), _resident((None, L, wg), kmap),
                  _resident((None, nb, wg, tq), lambda b, g, i: (b, 0, g, 0)),
                  pl.BlockSpec((None, tq, wg), qmap)],
        out_specs=pl.BlockSpec((None, tq, wg), qmap),
        out_shape=jax.ShapeDtypeStruct((B, T, W_C), BF16),
        scratch_shapes=[pltpu.VMEM((n_units, LANES, 2 * tq), F32)],
        compiler_params=_cparams("parallel", "parallel", "arbitrary"),
        name="moba_prompt",
    )(qt, kmean, k, vt, sg)


def _page_map(layer, n_pages):
    def index_map(b, p, pt):
        return (pt[b, jnp.minimum(p, n_pages - 1)], layer, 0, 0)
    return index_map


def _pad_rows(a, rows):
    return jnp.concatenate([a, jnp.zeros((rows - a.shape[0], a.shape[1]), a.dtype)], axis=0)


def _new_token_bias(nq):
    r = lax.broadcasted_iota(I32, (nq, LANES), 0)
    c = lax.broadcasted_iota(I32, (nq, LANES), 1)
    return jnp.where(c <= r, 0.0, NEG).astype(F32)


def _fold_heads(o, n_heads, nq, width):
    col_head = lax.broadcasted_iota(I32, (nq, n_heads * width), 1) // width
    out = jnp.zeros((nq, n_heads * width), F32)
    for h in range(n_heads):
        out = jnp.where(col_head == h, o[h * nq:(h + 1) * nq, :], out)
    return out


def _flash_ref_update(q, k, v, bias, m_ref, l_ref, acc_ref):
    m, l, acc = _flash_tile(q, k, v, bias, m_ref[...], l_ref[...], acc_ref[...])
    m_ref[...] = m
    l_ref[...] = l
    acc_ref[...] = acc


def _flash_ref_init(m_ref, l_ref, acc_ref):
    m_ref[...] = jnp.full(m_ref.shape, NEG, F32)
    l_ref[...] = jnp.zeros(l_ref.shape, F32)
    acc_ref[...] = jnp.zeros(acc_ref.shape, F32)


def _dsa_sample_select_kernel(pt_ref, qi_ref, wi_ref, kc_ref, kn_ref, bias_ref, key_ref, thr_ref, tie_ref,
                              *, n_pages, nq, topk, col_bits):
    p = pl.program_id(1)
    qi = qi_ref[...]
    wi = wi_ref[...]

    def score(kblk):
        rel = lax.dot_general(qi, kblk, (((1,), (1,)), ((), ())), preferred_element_type=F32)
        wr = wi * jnp.maximum(rel, 0.0)
        s = jnp.zeros((nq, kblk.shape[0]), F32)
        for h in range(H_IDX):
            s = s + wr[h * nq:(h + 1) * nq, :]
        return s

    @pl.when(p < n_pages)
    def _():
        key_ref[p] = _sortable_key(score(kc_ref[...].astype(BF16)))

    @pl.when(p == n_pages)
    def _():
        s = score(_pad_rows(kn_ref[...], LANES).astype(BF16))
        r = lax.broadcasted_iota(I32, (nq, LANES), 0)
        lane = lax.broadcasted_iota(I32, (nq, LANES), 1)
        key_ref[n_pages] = jnp.where(lane <= r, _sortable_key(s), INT_MIN)
        _topk_select(key_ref, thr_ref, tie_ref, n_pages + 1, nq, topk, col_bits, rb=nq)
        thr = thr_ref[...]
        tie = tie_ref[...]

        def bias_chunk(c, carry):
            bias_ref[c] = _select_bias(key_ref[c], thr, tie, c * LANES + lane)
            return carry

        lax.fori_loop(0, n_pages + 1, bias_chunk, 0)


def _dsa_sample_select(page_table, layer, qi_s, wi_s, cache_idx, ki_new):
    Bd, n_pages = page_table.shape
    nq = ki_new.shape[1]
    L = n_pages * cache_idx.shape[2] + nq
    topk = min(TOPK_MAX, L // 4)
    assert cache_idx.shape[2] == LANES and nq == 8
    bmap = lambda b, p, pt: (b, 0, 0)
    return pl.pallas_call(
        functools.partial(_dsa_sample_select_kernel, n_pages=n_pages, nq=nq, topk=topk,
                          col_bits=max(1, (L - 1).bit_length())),
        grid_spec=pltpu.PrefetchScalarGridSpec(
            num_scalar_prefetch=1, grid=(Bd, n_pages + 1),
            in_specs=[pl.BlockSpec((None,) + qi_s.shape[1:], bmap), pl.BlockSpec((None,) + wi_s.shape[1:], bmap),
                      pl.BlockSpec((None, None, LANES, D_IDX), _page_map(layer, n_pages)),
                      pl.BlockSpec((None, nq, D_IDX), bmap)],
            out_specs=pl.BlockSpec((None, n_pages + 1, nq, LANES), lambda b, p, pt: (b, 0, 0, 0)),
            scratch_shapes=[pltpu.VMEM((n_pages + 1, nq, LANES), I32), pltpu.VMEM((nq, LANES), I32),
                            pltpu.VMEM((nq, LANES), I32)]),
        out_shape=jax.ShapeDtypeStruct((Bd, n_pages + 1, nq, LANES), F32),
        compiler_params=_cparams("parallel", "arbitrary"),
        name="dsa_sample_select",
    )(page_table, qi_s, wi_s, cache_idx, ki_new)


def _dsa_sample_attn_kernel(pt_ref, q_ref, kvc_ref, kvn_ref, bias_ref, sg_ref, o_ref, m_ref, l_ref, acc_ref,
                            *, n_pages, nq):
    p = pl.program_id(1)
    q = q_ref[...]
    bias = jnp.concatenate([bias_ref[...]] * H_A, axis=0)

    @pl.when(p == 0)
    def _():
        _flash_ref_init(m_ref, l_ref, acc_ref)

    @pl.when(p < n_pages)
    def _():
        _flash_ref_update(q, kvc_ref[:, 0:W_A].astype(BF16), kvc_ref[:, W_A:2 * W_A].astype(BF16), bias,
                          m_ref, l_ref, acc_ref)

    @pl.when(p == n_pages)
    def _():
        kv = _pad_rows(kvn_ref[...], LANES).astype(BF16)
        _flash_ref_update(q, kv[:, 0:W_A], kv[:, W_A:2 * W_A], bias, m_ref, l_ref, acc_ref)
        o = _fold_heads(acc_ref[...] / l_ref[...], H_A, nq, HEAD_DIM)
        o_ref[...] = (o * sg_ref[...]).astype(BF16)


def _dsa_sample_attn(page_table, layer, q_bd, cache_kv, kv_new, bias, sg):
    Bd, n_pages = page_table.shape
    nq = kv_new.shape[1]
    rows = q_bd.shape[1]
    bmap = lambda b, p, pt: (b, 0, 0)
    return pl.pallas_call(
        functools.partial(_dsa_sample_attn_kernel, n_pages=n_pages, nq=nq),
        grid_spec=pltpu.PrefetchScalarGridSpec(
            num_scalar_prefetch=1, grid=(Bd, n_pages + 1),
            in_specs=[pl.BlockSpec((None, rows, W_A), bmap),
                      pl.BlockSpec((None, None, LANES, 2 * W_A), _page_map(layer, n_pages)),
                      pl.BlockSpec((None, nq, 2 * W_A), bmap),
                      pl.BlockSpec((None, None, nq, LANES), lambda b, p, pt: (b, p, 0, 0)),
                      pl.BlockSpec((None, nq, W_A), bmap)],
            out_specs=pl.BlockSpec((None, nq, W_A), bmap),
            scratch_shapes=[pltpu.VMEM((rows, 1), F32), pltpu.VMEM((rows, 1), F32), pltpu.VMEM((rows, W_A), F32)]),
        out_shape=jax.ShapeDtypeStruct((Bd, nq, W_A), BF16),
        compiler_params=_cparams("parallel", "arbitrary"),
        name="dsa_sample_attn",
    )(page_table, q_bd, cache_kv, kv_new, bias, sg)


def _diff_sample_kernel(pt_ref, q_ref, kvc_ref, kvn_ref, sg_ref, lam_ref, subg_ref, o_ref, m_ref, l_ref, acc_ref,
                        *, n_pages, nq, lam_init):
    p = pl.program_id(1)
    q = q_ref[...]

    @pl.when(p == 0)
    def _():
        _flash_ref_init(m_ref, l_ref, acc_ref)

    @pl.when(p < n_pages)
    def _():
        _flash_ref_update(q, kvc_ref[:, 0:W_B].astype(BF16), kvc_ref[:, W_B:2 * W_B].astype(BF16), None,
                          m_ref, l_ref, acc_ref)

    @pl.when(p == n_pages)
    def _():
        kv = _pad_rows(kvn_ref[...], LANES).astype(BF16)
        bias = jnp.concatenate([_new_token_bias(nq)] * (2 * H_B), axis=0)
        _flash_ref_update(q, kv[:, 0:W_B], kv[:, W_B:2 * W_B], bias, m_ref, l_ref, acc_ref)
        lam_val = _lambda_value(lam_ref, lam_init)
        dv = 2 * DH_B
        l = l_ref[...]
        for h in range(H_B):
            r1, r2 = (2 * h) * nq, (2 * h + 1) * nq
            o = _diff_finish(acc_ref[r1:r1 + nq, h * dv:(h + 1) * dv], l[r1:r1 + nq],
                             acc_ref[r2:r2 + nq, h * dv:(h + 1) * dv], l[r2:r2 + nq],
                             lam_val, subg_ref[...], sg_ref[:, h * dv:(h + 1) * dv], lam_init)
            o_ref[:, h * dv:(h + 1) * dv] = o.astype(BF16)


def _diff_sample(page_table, layer, q_bd, cache_kv, kv_new, sg, lam, subg, lam_init):
    Bd, n_pages = page_table.shape
    nq = kv_new.shape[1]
    rows = q_bd.shape[1]
    bmap = lambda b, p, pt: (b, 0, 0)
    const = lambda b, p, pt: (0, 0)
    return pl.pallas_call(
        functools.partial(_diff_sample_kernel, n_pages=n_pages, nq=nq, lam_init=lam_init),
        grid_spec=pltpu.PrefetchScalarGridSpec(
            num_scalar_prefetch=1, grid=(Bd, n_pages + 1),
            in_specs=[pl.BlockSpec((None, rows, W_B), bmap),
                      pl.BlockSpec((None, None, LANES, 2 * W_B), _page_map(layer, n_pages)),
                      pl.BlockSpec((None, nq, 2 * W_B), bmap),
                      pl.BlockSpec((None, nq, W_B), bmap),
                      pl.BlockSpec(lam.shape, const), pl.BlockSpec(subg.shape, const)],
            out_specs=pl.BlockSpec((None, nq, W_B), bmap),
            scratch_shapes=[pltpu.VMEM((rows, 1), F32), pltpu.VMEM((rows, 1), F32), pltpu.VMEM((rows, W_B), F32)]),
        out_shape=jax.ShapeDtypeStruct((Bd, nq, W_B), BF16),
        compiler_params=_cparams("parallel", "arbitrary"),
        name="diff_sample",
    )(page_table, q_bd, cache_kv, kv_new, sg, lam, subg)


def _moba_sample_gate_kernel(pt_ref, q_ref, kc_ref, sel_ref, ksum_ref, gate_ref, *, n_pages, ppb, n_sel):
    p = pl.program_id(1)
    colsum = jnp.sum(kc_ref[...], axis=0, keepdims=True)

    @pl.when(p % ppb == 0)
    def _():
        ksum_ref[...] = colsum

    @pl.when(p % ppb != 0)
    def _():
        ksum_ref[...] = ksum_ref[...] + colsum

    @pl.when(p == 0)
    def _():
        gate_ref[...] = jnp.zeros(gate_ref.shape, F32)

    @pl.when(p % ppb == ppb - 1)
    def _():
        mean = (ksum_ref[...] * (1.0 / MOBA_BLOCK)).astype(BF16)
        g = lax.dot_general(q_ref[...], jnp.broadcast_to(mean, (8, mean.shape[1])), (((1,), (1,)), ((), ())),
                            preferred_element_type=F32)
        lane = lax.broadcasted_iota(I32, gate_ref.shape, 1)
        gate_ref[...] = jnp.where(lane == p // ppb, g[:, 0:1], gate_ref[...])

    @pl.when(p == n_pages - 1)
    def _():
        sel_ref[...] = _top_blocks(gate_ref[...], n_pages // ppb, n_sel)


def _moba_sample_gate(page_table, layer, q_bd, cache_kv):
    Bd, n_pages = page_table.shape
    rows = q_bd.shape[1]
    ppb = MOBA_BLOCK // LANES
    nbp = n_pages // ppb
    assert n_pages % ppb == 0 and nbp <= LANES
    n_sel = max(1, min(MOBA_TOPB, nbp))
    bmap = lambda b, p, pt: (b, 0, 0)
    return pl.pallas_call(
        functools.partial(_moba_sample_gate_kernel, n_pages=n_pages, ppb=ppb, n_sel=n_sel),
        grid_spec=pltpu.PrefetchScalarGridSpec(
            num_scalar_prefetch=1, grid=(Bd, n_pages),
            in_specs=[pl.BlockSpec((None, rows, W_C), bmap),
                      pl.BlockSpec((None, None, LANES, W_C), _page_map(layer, n_pages))],
            out_specs=pl.BlockSpec((None, rows, LANES), bmap),
            scratch_shapes=[pltpu.VMEM((1, W_C), F32), pltpu.VMEM((rows, LANES), F32)]),
        out_shape=jax.ShapeDtypeStruct((Bd, rows, LANES), F32),
        compiler_params=_cparams("parallel", "arbitrary"),
        name="moba_sample_gate",
    )(page_table, q_bd, cache_kv)


def _moba_sample_attn_kernel(pt_ref, q_ref, kvc_ref, kvn_ref, sel_ref, sg_ref, o_ref, m_ref, l_ref, acc_ref,
                             *, n_pages, nq, ppb):
    p = pl.program_id(1)
    q = q_ref[...]

    @pl.when(p == 0)
    def _():
        _flash_ref_init(m_ref, l_ref, acc_ref)

    @pl.when(p < n_pages)
    def _():
        blk_row = lax.broadcasted_iota(I32, (LANES, LANES), 0)
        onehot = jnp.where(blk_row == p // ppb, 1.0, 0.0).astype(BF16)
        picked = jnp.dot(sel_ref[...].astype(BF16), onehot, preferred_element_type=F32)
        bias = (picked - 1.0) * (-NEG)
        _flash_ref_update(q, kvc_ref[:, 0:W_C].astype(BF16), kvc_ref[:, W_C:2 * W_C].astype(BF16), bias,
                          m_ref, l_ref, acc_ref)

    @pl.when(p == n_pages)
    def _():
        kv = _pad_rows(kvn_ref[...], LANES).astype(BF16)
        bias = jnp.concatenate([_new_token_bias(nq)] * H_C, axis=0)
        _flash_ref_update(q, kv[:, 0:W_C], kv[:, W_C:2 * W_C], bias, m_ref, l_ref, acc_ref)
        o = _fold_heads(acc_ref[...] / l_ref[...], H_C, nq, HEAD_DIM)
        o_ref[...] = (o * sg_ref[...]).astype(BF16)


def _moba_sample_attn(page_table, layer, q_bd, cache_kv, kv_new, sel, sg):
    Bd, n_pages = page_table.shape
    nq = kv_new.shape[1]
    rows = q_bd.shape[1]
    bmap = lambda b, p, pt: (b, 0, 0)
    return pl.pallas_call(
        functools.partial(_moba_sample_attn_kernel, n_pages=n_pages, nq=nq, ppb=MOBA_BLOCK // LANES),
        grid_spec=pltpu.PrefetchScalarGridSpec(
            num_scalar_prefetch=1, grid=(Bd, n_pages + 1),
            in_specs=[pl.BlockSpec((None, rows, W_C), bmap),
                      pl.BlockSpec((None, None, LANES, 2 * W_C), _page_map(layer, n_pages)),
                      pl.BlockSpec((None, nq, 2 * W_C), bmap),
                      pl.BlockSpec((None, rows, LANES), bmap),
                      pl.BlockSpec((None, nq, W_C), bmap)],
            out_specs=pl.BlockSpec((None, nq, W_C), bmap),
            scratch_shapes=[pltpu.VMEM((rows, 1), F32), pltpu.VMEM((rows, 1), F32), pltpu.VMEM((rows, W_C), F32)]),
        out_shape=jax.ShapeDtypeStruct((Bd, nq, W_C), BF16),
        compiler_params=_cparams("parallel", "arbitrary"),
        name="moba_sample_attn",
    )(page_table, q_bd, cache_kv, kv_new, sel, sg)


def _block_diag_queries(q2, Bd, nq, n_heads, width):
    q4 = q2.reshape(Bd, nq, n_heads, width)
    eye = jnp.eye(n_heads, dtype=q2.dtype)
    return jnp.einsum('bqhd,hg->bhqgd', q4, eye).reshape(Bd, n_heads * nq, n_heads * width)


def _sample_trunk(x, past_len, caches, page_table, norm_g, w_even, w_out_even, lam_even, subln_g_even,
                  w_odd, w_out_odd, final_norm_g):
    cache_a_kv, cache_a_idx, cache_b_kv, cache_c_kv = caches
    Bd, nq, D = x.shape
    M = Bd * nq
    n_pool, _, page, _ = cache_a_idx.shape
    assert page == LANES and past_len % MOBA_BLOCK == 0 and nq <= 8
    ca_kv = cache_a_kv.reshape(n_pool, cache_a_kv.shape[1], page, 2 * W_A)
    cb_kv = cache_b_kv.reshape(n_pool, cache_b_kv.shape[1], page, 2 * W_B)
    cc_kv = cache_c_kv.reshape(n_pool, cache_c_kv.shape[1], page, 2 * W_C)
    depth = norm_g.shape[0]
    cos, sin = _rope_tables(past_len + jnp.arange(nq, dtype=I32))
    cos, sin = jnp.tile(cos, (Bd, 1)), jnp.tile(sin, (Bd, 1))
    x2 = x.reshape(M, D)
    a_kv, a_idx, b_kv, c_kv = [], [], [], []
    y = None
    r3 = lambda a: a.reshape(Bd, nq, a.shape[1])
    for li in range(depth):
        j = li // 2
        g = norm_g[li][None, :]
        last = li == depth - 1
        gf = final_norm_g[None, :] if last else None
        if li % 2 == 0:
            lam_init = 0.8 - 0.6 * math.exp(-0.3 * li)
            (qa, kva, _, _, sg, qi, kif, _, wi, qb, kvb, _, _) = _even_proj(x2, g, *w_even[j], cos, sin, M)
            qi_s = qi.reshape(Bd, nq, H_IDX, D_IDX).transpose(0, 2, 1, 3).reshape(Bd, H_IDX * nq, D_IDX)
            wi_s = wi.reshape(Bd, nq, H_IDX).transpose(0, 2, 1).reshape(Bd, H_IDX * nq, 1)
            bias = _dsa_sample_select(page_table, j, qi_s, wi_s, cache_a_idx, r3(kif))
            oa = _dsa_sample_attn(page_table, j, _block_diag_queries(qa, Bd, nq, H_A, HEAD_DIM), ca_kv,
                                  r3(kva), bias, r3(sg[:, :W_A]))
            ob = _diff_sample(page_table, j, _block_diag_queries(qb, Bd, nq, 2 * H_B, DH_B), cb_kv, r3(kvb),
                              r3(sg[:, W_A:]), lam_even[j], subln_g_even[j][None, :], lam_init)
            res = _out_proj([oa.reshape(M, W_A), ob.reshape(M, W_B)], w_out_even[j], x2, gf, M)
            a_kv.append(kva.reshape(Bd, nq, 2, H_A, HEAD_DIM))
            a_idx.append(kif.reshape(Bd, nq, D_IDX))
            b_kv.append(kvb.reshape(Bd, nq, 2, H_B, 2 * DH_B))
        else:
            q, kv, _, _, sg, _ = _odd_proj(x2, g, w_odd[j], cos, sin, M)
            q_bd = _block_diag_queries(q, Bd, nq, H_C, HEAD_DIM)
            sel = _moba_sample_gate(page_table, j, q_bd, cc_kv)
            o = _moba_sample_attn(page_table, j, q_bd, cc_kv, r3(kv), sel, r3(sg))
            res = _out_proj([o.reshape(M, W_C)], w_out_odd[j], x2, gf, M)
            c_kv.append(kv.reshape(Bd, nq, 2, H_C, HEAD_DIM))
        x2 = res[0]
        if last:
            y = res[1]
    return (y.reshape(Bd, nq, D), jnp.stack(a_kv, axis=1), jnp.stack(a_idx, axis=1),
            jnp.stack(b_kv, axis=1), jnp.stack(c_kv, axis=1))


def _rope_tables(pos):
    half = HEAD_DIM // 2
    inv_freq = jnp.exp(-math.log(ROPE_THETA) * jnp.arange(half, dtype=F32) / half)
    ang = pos.astype(F32)[:, None] * inv_freq[None, :]
    cos, sin = jnp.cos(ang), jnp.sin(ang)
    return jnp.concatenate([cos, cos, cos, cos], axis=1), jnp.concatenate([-sin, sin, -sin, sin], axis=1)


def _split_even_weight(w):
    wa = w[:, 0:4 * W_A].astype(BF16)
    wi = jnp.pad(w[:, 4 * W_A:4 * W_A + IDX_COLS], ((0, 0), (0, IDX_PAD - IDX_COLS))).astype(BF16)
    wb = w[:, 4 * W_A + IDX_COLS:].astype(BF16)
    return wa, wi, wb


def _prompt_trunk(x, norm_g, w_even, w_out_even, lam_even, subln_g_even, w_odd, w_out_odd, final_norm_g):
    B, T, D = x.shape
    M = B * T
    tm = min(256, T)
    depth = norm_g.shape[0]
    cos, sin = _rope_tables(jnp.arange(T, dtype=I32))
    cos, sin = jnp.tile(cos, (B, 1)), jnp.tile(sin, (B, 1))
    x2 = x.reshape(M, D)
    a_kv, a_idx, b_kv, c_kv = [], [], [], []
    y = None
    r3 = lambda a: a.reshape(B, T, a.shape[1])
    tr = lambda a: r3(a).transpose(0, 2, 1)
    tile_tr = lambda a: a.reshape(B, T // tm, tm, a.shape[1]).transpose(0, 1, 3, 2)
    for li in range(depth):
        j = li // 2
        g = norm_g[li][None, :]
        last = li == depth - 1
        gf = final_norm_g[None, :] if last else None
        if li % 2 == 0:
            lam_init = 0.8 - 0.6 * math.exp(-0.3 * li)
            (qa, kva, ka, va, sg, qi, kif, kib, wi, qb, kvb, kb, vb) = _even_proj(
                x2, g, *w_even[j], cos, sin, tm)
            oa = _dsa_prompt(tr(qa), tr(qi), tr(wi), r3(kib), r3(ka), tile_tr(va), r3(sg[:, :W_A]), tm)
            ob = _diff_prompt(tr(qb), r3(kb), tile_tr(vb), r3(sg[:, W_A:]), lam_even[j],
                              subln_g_even[j][None, :], lam_init, tm)
            res = _out_proj([oa.reshape(M, W_A), ob.reshape(M, W_B)], w_out_even[j], x2, gf, tm)
            a_kv.append(kva.reshape(B, T, 2, H_A, HEAD_DIM))
            a_idx.append(kif.reshape(B, T, D_IDX))
            b_kv.append(kvb.reshape(B, T, 2, H_B, 2 * DH_B))
        else:
            assert T % MOBA_BLOCK == 0
            q, kv, k, v, sg, ksum = _odd_proj(x2, g, w_odd[j], cos, sin, MOBA_BLOCK)
            nb = T // MOBA_BLOCK
            kmean = ksum.reshape(B, nb, W_C) * (1.0 / MOBA_BLOCK)
            kmean = jnp.pad(kmean, ((0, 0), (0, LANES - nb), (0, 0)))
            o = _moba_prompt(tr(q), kmean, r3(k), tile_tr(v), r3(sg))
            res = _out_proj([o.reshape(M, W_C)], w_out_odd[j], x2, gf, tm)
            c_kv.append(kv.reshape(B, T, 2, H_C, HEAD_DIM))
        x2 = res[0]
        if last:
            y = res[1]
    return (y.reshape(B, T, D), jnp.stack(a_kv, axis=1), jnp.stack(a_idx, axis=1),
            jnp.stack(b_kv, axis=1), jnp.stack(c_kv, axis=1))


def kernel(x_prompt, x_sample, cache_a_kv, cache_a_idx, cache_b_kv, cache_c_kv, page_table, norm_g,
           w_in_even, w_out_even, lam_even, subln_g_even, w_in_odd, w_out_odd, final_norm_g):
    w_even = [_split_even_weight(w_in_even[j]) for j in range(w_in_even.shape[0])]
    w_odd = [w_in_odd[j].astype(BF16) for j in range(w_in_odd.shape[0])]
    w_out_e = [w_out_even[j].astype(BF16) for j in range(w_out_even.shape[0])]
    w_out_o = [w_out_odd[j].astype(BF16) for j in range(w_out_odd.shape[0])]
    weights = (norm_g, w_even, w_out_e, lam_even, subln_g_even, w_odd, w_out_o, final_norm_g)
    y_p, a_kv_p, a_idx_p, b_kv_p, c_kv_p = _prompt_trunk(x_prompt, *weights)
    past_len = page_table.shape[1] * cache_a_idx.shape[2]
    y_s, a_kv_s, a_idx_s, b_kv_s, c_kv_s = _sample_trunk(
        x_sample, past_len, (cache_a_kv, cache_a_idx, cache_b_kv, cache_c_kv), page_table, *weights)
    return (y_p, y_s, a_kv_p, a_idx_p, b_kv_p, c_kv_p, a_kv_s, a_idx_s, b_kv_s, c_kv_s)
```

```python
import functools
import math

import jax
import jax.numpy as jnp
from jax import lax
from jax.experimental import pallas as pl
from jax.experimental.pallas import tpu as pltpu

F32 = jnp.float32
BF16 = jnp.bfloat16
I32 = jnp.int32

HEAD_DIM = 64
H_A = 8
H_IDX = 8
D_IDX = 64
TOPK_MAX = 256
H_B = 4
DH_B = 64
H_C = 16
MOBA_BLOCK = 256
MOBA_TOPB = 3
ROPE_THETA = 10000.0
NORM_EPS = 1e-6
SUBLN_EPS = 1e-5
W_A = H_A * HEAD_DIM
W_B = H_B * 2 * DH_B
W_C = H_C * HEAD_DIM
IDX_COLS = H_IDX * D_IDX + D_IDX + H_IDX
IDX_PAD = 640
LANES = 128
NEG = -1e30
INT_MIN = -2 ** 31
VMEM_LIMIT = 56 * 1024 * 1024


def _cparams(*sem):
    return pltpu.CompilerParams(dimension_semantics=sem, vmem_limit_bytes=VMEM_LIMIT)


def _resident(block_shape, index_map):
    return pl.BlockSpec(block_shape, index_map, pipeline_mode=pl.Buffered(1))


def _rms(x, g, eps):
    ms = jnp.mean(x * x, axis=-1, keepdims=True)
    return x * lax.rsqrt(ms + eps) * g


def _silu(g):
    return g / (1.0 + jnp.exp(-g))


def _rope128(xc, cos, sin, first_half):
    sw = jnp.where(first_half, pltpu.roll(xc, 96, 1), pltpu.roll(xc, 32, 1))
    return xc * cos + sw * sin


def _rope_wide(y, cos, sin, first_half):
    return jnp.concatenate(
        [_rope128(y[:, c * LANES:(c + 1) * LANES], cos, sin, first_half) for c in range(y.shape[1] // LANES)],
        axis=1)


def _first_half_mask(rows):
    lane = lax.broadcasted_iota(I32, (rows, LANES), 1)
    return (lane % HEAD_DIM) < (HEAD_DIM // 2)


def _flash_tile(q, k, v, bias, m, l, acc):
    s = lax.dot_general(q, k, (((1,), (1,)), ((), ())), preferred_element_type=F32)
    if bias is not None:
        s = s + bias
    m_new = jnp.maximum(m, jnp.max(s, axis=1, keepdims=True))
    alpha = jnp.exp(m - m_new)
    p = jnp.exp(s - m_new)
    l = alpha * l + jnp.sum(p, axis=1, keepdims=True)
    acc = alpha * acc + jnp.dot(p.astype(BF16), v, preferred_element_type=F32)
    return m_new, l, acc


def _causal_bias(rows, cols):
    r = lax.broadcasted_iota(I32, (rows, cols), 0)
    c = lax.broadcasted_iota(I32, (rows, cols), 1)
    return jnp.where(c <= r, 0.0, NEG).astype(F32)


def _sortable_key(score):
    bits = lax.bitcast_convert_type(score, I32)
    return jnp.where(bits < 0, bits ^ jnp.int32(0x7FFFFFFF), bits)


def _even_proj_kernel(x_ref, g_ref, wa_ref, wi_ref, wb_ref, cos_ref, sin_ref,
                      qa_ref, kva_ref, ka_ref, va_ref, sg_ref, qi_ref, kif_ref, kib_ref, wio_ref,
                      qb_ref, kvb_ref, kb_ref, vb_ref):
    h = _rms(x_ref[...], g_ref[...], NORM_EPS).astype(BF16)
    cos = cos_ref[...]
    sin = sin_ref[...]
    fh = _first_half_mask(h.shape[0])
    scale = HEAD_DIM ** -0.5

    def proj(w_ref, lo, hi):
        return jnp.dot(h, w_ref[:, lo:hi], preferred_element_type=F32)

    for w_ref, q_ref, kv_ref, k_ref, v_ref, g_lo in ((wa_ref, qa_ref, kva_ref, ka_ref, va_ref, 0),
                                                      (wb_ref, qb_ref, kvb_ref, kb_ref, vb_ref, W_A)):
        q = _rope_wide(proj(w_ref, 0, 512), cos, sin, fh)
        q_ref[...] = (q * scale).astype(BF16)
        k = _rope_wide(proj(w_ref, 512, 1024), cos, sin, fh)
        v = proj(w_ref, 1024, 1536)
        kv_ref[:, 0:512] = k
        kv_ref[:, 512:1024] = v
        k_ref[...] = k.astype(BF16)
        v_ref[...] = v.astype(BF16)
        sg_ref[:, g_lo:g_lo + 512] = _silu(proj(w_ref, 1536, 2048))

    qi_ref[...] = _rope_wide(proj(wi_ref, 0, 512), cos, sin, fh).astype(BF16)
    tail = proj(wi_ref, 512, IDX_PAD)
    ki = _rope128(tail, cos, sin, fh)[:, 0:D_IDX]
    kif_ref[...] = ki
    kib_ref[...] = ki.astype(BF16)
    wio_ref[...] = tail[:, D_IDX:D_IDX + H_IDX]


def _even_proj(x2, g, wa, wi, wb, cos, sin, tm):
    M, D = x2.shape
    row = lambda i: (i, 0)
    const = lambda i: (0, 0)
    out_cols = ((512, BF16), (1024, F32), (512, BF16), (512, BF16), (1024, F32), (512, BF16), (D_IDX, F32),
                (D_IDX, BF16), (H_IDX, F32), (512, BF16), (1024, F32), (512, BF16), (512, BF16))
    return pl.pallas_call(
        _even_proj_kernel,
        grid=(M // tm,),
        in_specs=[pl.BlockSpec((tm, D), row), pl.BlockSpec((1, D), const),
                  _resident(wa.shape, const), _resident(wi.shape, const), _resident(wb.shape, const),
                  pl.BlockSpec((tm, LANES), row), pl.BlockSpec((tm, LANES), row)],
        out_specs=[pl.BlockSpec((tm, c), row) for c, _ in out_cols],
        out_shape=[jax.ShapeDtypeStruct((M, c), dt) for c, dt in out_cols],
        compiler_params=_cparams("parallel"),
        name="even_proj",
    )(x2, g, wa, wi, wb, cos, sin)


def _odd_proj_kernel(x_ref, g_ref, w_ref, cos_ref, sin_ref, q_ref, kv_ref, k_ref, v_ref, sg_ref, ksum_ref):
    h = _rms(x_ref[...], g_ref[...], NORM_EPS).astype(BF16)
    cos = cos_ref[...]
    sin = sin_ref[...]
    fh = _first_half_mask(h.shape[0])
    scale = HEAD_DIM ** -0.5
    for c in range(W_C // 512):
        lo = c * 512
        q = _rope_wide(jnp.dot(h, w_ref[:, lo:lo + 512], preferred_element_type=F32), cos, sin, fh)
        q_ref[:, lo:lo + 512] = (q * scale).astype(BF16)
        k = _rope_wide(jnp.dot(h, w_ref[:, W_C + lo:W_C + lo + 512], preferred_element_type=F32), cos, sin, fh)
        kv_ref[:, lo:lo + 512] = k
        k_ref[:, lo:lo + 512] = k.astype(BF16)
        ksum_ref[:, lo:lo + 512] = jnp.sum(k, axis=0, keepdims=True)
        v = jnp.dot(h, w_ref[:, 2 * W_C + lo:2 * W_C + lo + 512], preferred_element_type=F32)
        kv_ref[:, W_C + lo:W_C + lo + 512] = v
        v_ref[:, lo:lo + 512] = v.astype(BF16)
        sg_ref[:, lo:lo + 512] = _silu(jnp.dot(h, w_ref[:, 3 * W_C + lo:3 * W_C + lo + 512],
                                               preferred_element_type=F32))


def _odd_proj(x2, g, w, cos, sin, tm):
    M, D = x2.shape
    row = lambda i: (i, 0)
    const = lambda i: (0, 0)
    out_cols = ((W_C, BF16), (2 * W_C, F32), (W_C, BF16), (W_C, BF16), (W_C, F32))
    return pl.pallas_call(
        _odd_proj_kernel,
        grid=(M // tm,),
        in_specs=[pl.BlockSpec((tm, D), row), pl.BlockSpec((1, D), const), _resident(w.shape, const),
                  pl.BlockSpec((tm, LANES), row), pl.BlockSpec((tm, LANES), row)],
        out_specs=[pl.BlockSpec((tm, c), row) for c, _ in out_cols]
                  + [pl.BlockSpec((None, 1, W_C), lambda i: (i, 0, 0))],
        out_shape=[jax.ShapeDtypeStruct((M, c), dt) for c, dt in out_cols]
                  + [jax.ShapeDtypeStruct((M // tm, 1, W_C), F32)],
        compiler_params=_cparams("parallel"),
        name="odd_proj",
    )(x2, g, w, cos, sin)


def _out_proj_kernel(*refs, n_in, final):
    m_refs = refs[:n_in]
    w_ref, x_ref = refs[n_in], refs[n_in + 1]
    acc = x_ref[...]
    lo = 0
    for m_ref in m_refs:
        kk = m_ref.shape[1]
        acc = acc + jnp.dot(m_ref[...], w_ref[lo:lo + kk, :], preferred_element_type=F32)
        lo += kk
    if final:
        gf_ref, o_ref, y_ref = refs[n_in + 2:]
        y_ref[...] = _rms(acc, gf_ref[...], NORM_EPS)
    else:
        o_ref = refs[n_in + 2]
    o_ref[...] = acc


def _out_proj(mixed, w, x2, gf, tm):
    M, D = x2.shape
    row = lambda i: (i, 0)
    const = lambda i: (0, 0)
    final = gf is not None
    in_specs = [pl.BlockSpec((tm, m.shape[1]), row) for m in mixed]
    in_specs += [_resident(w.shape, const), pl.BlockSpec((tm, D), row)]
    args = list(mixed) + [w, x2]
    out_specs = [pl.BlockSpec((tm, D), row)]
    out_shape = [jax.ShapeDtypeStruct((M, D), F32)]
    if final:
        in_specs.append(pl.BlockSpec((1, D), const))
        args.append(gf)
        out_specs.append(pl.BlockSpec((tm, D), row))
        out_shape.append(jax.ShapeDtypeStruct((M, D), F32))
    return pl.pallas_call(
        functools.partial(_out_proj_kernel, n_in=len(mixed), final=final),
        grid=(M // tm,), in_specs=in_specs, out_specs=out_specs, out_shape=out_shape,
        compiler_params=_cparams("parallel"),
        name="out_proj",
    )(*args)


def _topk_select(key_ref, thr_ref, tie_ref, nch, rows, topk, col_bits, rb):
    for r0 in range(0, rows, rb):
        def count(pred):
            def body(c, acc):
                return acc + pred(key_ref[c, r0:r0 + rb, :], c).astype(I32)
            acc = lax.fori_loop(0, nch, body, jnp.zeros((rb, LANES), I32))
            return jnp.sum(acc, axis=1, keepdims=True)

        def count_ge(cand):
            cb = jnp.broadcast_to(cand, (rb, LANES))
            return count(lambda blk, c: blk >= cb)

        t0 = jnp.full((rb, 1), INT_MIN, I32)
        t0 = jnp.where(count_ge(jnp.zeros((rb, 1), I32)) >= topk, 0, t0)

        def bit_step(i, t):
            cand = t | jnp.left_shift(jnp.int32(1), 30 - i)
            return jnp.where(count_ge(cand) >= topk, cand, t)

        t = lax.fori_loop(0, 31, bit_step, t0)
        tb = jnp.broadcast_to(t, (rb, LANES))
        n_gt = count(lambda blk, c: blk > tb)
        n_ge = count(lambda blk, c: blk >= tb)
        need = topk - n_gt
        thr_ref[r0:r0 + rb, :] = tb
        tie_ref[r0:r0 + rb, :] = jnp.full((rb, LANES), 2 ** 30, I32)

        @pl.when(jnp.max(n_ge) > topk)
        def _():
            lane = lax.broadcasted_iota(I32, (rb, LANES), 1)

            def tie_step(i, vmax):
                cand = vmax | jnp.left_shift(jnp.int32(1), col_bits - 1 - i)
                cb = jnp.broadcast_to(cand, (rb, LANES))
                below = count(lambda blk, c: (blk == tb) & (c * LANES + lane < cb))
                return jnp.where(below < need, cand, vmax)

            tie = lax.fori_loop(0, col_bits, tie_step, jnp.zeros((rb, 1), I32))
            tie = jnp.where(n_ge > topk, tie, 2 ** 30)
            tie_ref[r0:r0 + rb, :] = jnp.broadcast_to(tie, (rb, LANES))


def _select_bias(key, thr, tie, col):
    sel = (key > thr) | ((key == thr) & (col <= tie) & (key != INT_MIN))
    return jnp.where(sel, 0.0, NEG).astype(F32)


def _block_diag_qt(qt_pair, tq):
    z = jnp.zeros((HEAD_DIM, tq), qt_pair.dtype)
    return jnp.concatenate([jnp.concatenate([qt_pair[0:HEAD_DIM], z], axis=1),
                            jnp.concatenate([z, qt_pair[HEAD_DIM:2 * HEAD_DIM]], axis=1)], axis=0)


def _flash_tile_t(s, vts, m, l, acc):
    tq = s.shape[1] // 2
    m_new = jnp.maximum(m, jnp.max(s, axis=0, keepdims=True))
    alpha = jnp.exp(m - m_new)
    p = jnp.exp(s - m_new)
    l = alpha * l + jnp.sum(p, axis=0, keepdims=True)
    pb = p.astype(BF16)
    if len(vts) == 1:
        pv = jnp.dot(vts[0], pb, preferred_element_type=F32)
    else:
        pv = jnp.concatenate([jnp.dot(vts[0], pb[:, :tq], preferred_element_type=F32),
                              jnp.dot(vts[1], pb[:, tq:], preferred_element_type=F32)], axis=1)
    return m_new, l, alpha * acc + pv


def _flash_init_t(dv, tq):
    return (jnp.full((1, 2 * tq), NEG, F32), jnp.zeros((1, 2 * tq), F32), jnp.zeros((dv, 2 * tq), F32))


def _causal_bias_t(tq):
    key = lax.broadcasted_iota(I32, (tq, tq), 0)
    qry = lax.broadcasted_iota(I32, (tq, tq), 1)
    b = jnp.where(key <= qry, 0.0, NEG).astype(F32)
    return jnp.concatenate([b, b], axis=1)


def _pair_to_rows(o, tq):
    return jnp.concatenate([o[:, :tq], o[:, tq:]], axis=0).T


def _topk_select_t(key_ref, n_rows, tq, topk, col_bits):
    rb = tq
    n_acc = 4
    nblk = n_rows // rb
    rowi = lax.broadcasted_iota(I32, (8, tq), 0)

    def count(pred):
        def body(c, accs):
            r0 = pl.multiple_of(c * rb, rb)
            accs = list(accs)
            blk = key_ref[pl.ds(r0, rb), :]
            for g in range(rb // 8):
                hit = pred(blk[8 * g:8 * g + 8], r0 + 8 * g)
                accs[g % n_acc] = accs[g % n_acc] + hit.astype(I32)
            return tuple(accs)
        accs = lax.fori_loop(0, nblk, body, tuple(jnp.zeros((8, tq), I32) for _ in range(n_acc)))
        return jnp.sum(sum(accs[1:], accs[0]), axis=0, keepdims=True)

    def count_ge(cand):
        cb = jnp.broadcast_to(cand, (8, tq))
        return count(lambda blk, r0: blk >= cb)

    t0 = jnp.where(count_ge(jnp.zeros((1, tq), I32)) >= topk, 0, INT_MIN).astype(I32)

    def bit_step(b, t):
        cand = t | jnp.left_shift(jnp.int32(1), 30 - b)
        return jnp.where(count_ge(cand) >= topk, cand, t)

    thr = lax.fori_loop(0, 31, bit_step, t0)
    tb = jnp.broadcast_to(thr, (8, tq))
    n_gt = count(lambda blk, r0: blk > tb)
    n_ge = count(lambda blk, r0: blk >= tb)
    need = topk - n_gt

    def tie_search():
        def tie_step(b, vmax):
            cand = vmax | jnp.left_shift(jnp.int32(1), col_bits - 1 - b)
            cb = jnp.broadcast_to(cand, (8, tq))
            below = count(lambda blk, r0: (blk == tb) & (r0 + rowi < cb))
            return jnp.where(below < need, cand, vmax)
        tie = lax.fori_loop(0, col_bits, tie_step, jnp.zeros((1, tq), I32))
        return jnp.where(n_ge > topk, tie, 2 ** 30)

    tie = lax.cond(jnp.max(n_ge) > topk, tie_search, lambda: jnp.full((1, tq), 2 ** 30, I32))
    return thr, tie


def _dsa_prompt_kernel(qat_ref, qit_ref, wit_ref, ki_ref, ka_ref, vat_ref, sg_ref, o_ref, key_ref,
                       *, tq, topk, col_bits):
    i = pl.program_id(1)
    nkt = i + 1
    n_units = W_A // LANES
    keyi = lax.broadcasted_iota(I32, (tq, tq), 0)
    qryi = lax.broadcasted_iota(I32, (tq, tq), 1)
    wit = wit_ref[...]

    def score_tile(kt, carry):
        start = pl.multiple_of(kt * tq, tq)
        ki = ki_ref[pl.ds(start, tq), :]
        score = jnp.zeros((tq, tq), F32)
        for h in range(H_IDX):
            rel = jnp.dot(ki, qit_ref[h * D_IDX:(h + 1) * D_IDX, :], preferred_element_type=F32)
            score = score + wit[h:h + 1, :] * jnp.maximum(rel, 0.0)
        key = _sortable_key(score)
        key_ref[pl.ds(start, tq), :] = jnp.where((kt < i) | (keyi <= qryi), key, INT_MIN)
        return carry

    lax.fori_loop(0, nkt, score_tile, 0)
    thr, tie = _topk_select_t(key_ref, nkt * tq, tq, topk, col_bits)
    qbd = [_block_diag_qt(qat_ref[u * LANES:(u + 1) * LANES, :], tq) for u in range(n_units)]

    def kv_tile(kt, carry):
        start = pl.multiple_of(kt * tq, tq)
        key = key_ref[pl.ds(start, tq), :]
        sel = (key > thr) | ((key == thr) & (kt * tq + keyi <= tie) & (key != INT_MIN))
        bias = jnp.where(sel, 0.0, NEG).astype(F32)
        bias = jnp.concatenate([bias, bias], axis=1)
        out = []
        for u in range(n_units):
            s = jnp.dot(ka_ref[pl.ds(start, tq), u * LANES:(u + 1) * LANES], qbd[u],
                        preferred_element_type=F32) + bias
            vts = [vat_ref[kt, u * LANES:u * LANES + HEAD_DIM, :],
                   vat_ref[kt, u * LANES + HEAD_DIM:(u + 1) * LANES, :]]
            out.append(_flash_tile_t(s, vts, *carry[u]))
        return tuple(out)

    carry = lax.fori_loop(0, nkt, kv_tile, tuple(_flash_init_t(HEAD_DIM, tq) for _ in range(n_units)))
    for u in range(n_units):
        _, l, acc = carry[u]
        o = _pair_to_rows(acc / l, tq) * sg_ref[:, u * LANES:(u + 1) * LANES]
        o_ref[:, u * LANES:(u + 1) * LANES] = o.astype(BF16)


def _dsa_prompt(qat, qit, wit, ki, ka, vat, sg, tq):
    B, _, T = qat.shape
    L = ka.shape[1]
    topk = min(TOPK_MAX, L // 4)
    assert T == L and T % tq == 0 and tq % LANES == 0
    qtmap = lambda b, i: (b, 0, i)
    kmap = lambda b, i: (b, 0, 0)
    return pl.pallas_call(
        functools.partial(_dsa_prompt_kernel, tq=tq, topk=topk, col_bits=max(1, (L - 1).bit_length())),
        grid=(B, T // tq),
        in_specs=[pl.BlockSpec((None, W_A, tq), qtmap), pl.BlockSpec((None, H_IDX * D_IDX, tq), qtmap),
                  pl.BlockSpec((None, H_IDX, tq), qtmap),
                  _resident((None, L, D_IDX), kmap), _resident((None, L, W_A), kmap),
                  _resident((None, L // tq, W_A, tq), lambda b, i: (b, 0, 0, 0)),
                  pl.BlockSpec((None, tq, W_A), lambda b, i: (b, i, 0))],
        out_specs=pl.BlockSpec((None, tq, W_A), lambda b, i: (b, i, 0)),
        out_shape=jax.ShapeDtypeStruct((B, T, W_A), BF16),
        scratch_shapes=[pltpu.VMEM((L, tq), I32)],
        compiler_params=_cparams("parallel", "arbitrary"),
        name="dsa_prompt",
    )(qat, qit, wit, ki, ka, vat, sg)


def _lambda_value(lam_ref, lam_init):
    lam = lam_ref[...]
    s1 = jnp.sum(lam[0:1] * lam[1:2], axis=1, keepdims=True)
    s2 = jnp.sum(lam[2:3] * lam[3:4], axis=1, keepdims=True)
    return jnp.exp(s1) - jnp.exp(s2) + lam_init


def _diff_finish(acc1, l1, acc2, l2, lam_val, subg, sg, lam_init):
    o = acc1 / l1 - lam_val * (acc2 / l2)
    return _rms(o, subg, SUBLN_EPS) * (1.0 - lam_init) * sg


def _diff_prompt_kernel(qt_ref, k_ref, vt_ref, sg_ref, lam_ref, subg_ref, o_ref, *, tq, lam_init):
    i = pl.program_id(1)
    lam_val = _lambda_value(lam_ref, lam_init)
    diag = _causal_bias_t(tq)
    dv = 2 * DH_B
    qbd = [_block_diag_qt(qt_ref[h * dv:(h + 1) * dv, :], tq) for h in range(H_B)]

    def tile(kt, carry, bias):
        start = pl.multiple_of(kt * tq, tq)
        out = []
        for h in range(H_B):
            s = jnp.dot(k_ref[pl.ds(start, tq), h * dv:(h + 1) * dv], qbd[h], preferred_element_type=F32)
            if bias is not None:
                s = s + bias
            out.append(_flash_tile_t(s, [vt_ref[kt, h * dv:(h + 1) * dv, :]], *carry[h]))
        return tuple(out)

    carry = lax.fori_loop(0, i, lambda kt, c: tile(kt, c, None), tuple(_flash_init_t(dv, tq) for _ in range(H_B)))
    carry = tile(i, carry, diag)
    for h in range(H_B):
        _, l, acc = carry[h]
        o = acc / l
        o = (o[:, :tq] - lam_val * o[:, tq:]).T
        o = _rms(o, subg_ref[...], SUBLN_EPS) * (1.0 - lam_init) * sg_ref[:, h * dv:(h + 1) * dv]
        o_ref[:, h * dv:(h + 1) * dv] = o.astype(BF16)


def _diff_prompt(qbt, kb, vbt, sg, lam, subg, lam_init, tq):
    B, _, T = qbt.shape
    L = kb.shape[1]
    assert T == L and T % tq == 0
    qmap = lambda b, i: (b, i, 0)
    kmap = lambda b, i: (b, 0, 0)
    const = lambda b, i: (0, 0)
    return pl.pallas_call(
        functools.partial(_diff_prompt_kernel, tq=tq, lam_init=lam_init),
        grid=(B, T // tq),
        in_specs=[pl.BlockSpec((None, W_B, tq), lambda b, i: (b, 0, i)), _resident((None, L, W_B), kmap),
                  _resident((None, L // tq, W_B, tq), lambda b, i: (b, 0, 0, 0)),
                  pl.BlockSpec((None, tq, W_B), qmap),
                  pl.BlockSpec(lam.shape, const), pl.BlockSpec(subg.shape, const)],
        out_specs=pl.BlockSpec((None, tq, W_B), qmap),
        out_shape=jax.ShapeDtypeStruct((B, T, W_B), BF16),
        compiler_params=_cparams("parallel", "arbitrary"),
        name="diff_prompt",
    )(qbt, kb, vbt, sg, lam, subg)


def _top_blocks(gate, n_valid, n_sel):
    lane = lax.broadcasted_iota(I32, gate.shape, 1)
    g = jnp.where(lane < n_valid, gate, -jnp.inf)
    sel = jnp.zeros(gate.shape, F32)
    for _ in range(n_sel):
        mx = jnp.max(g, axis=1, keepdims=True)
        first = jnp.min(jnp.where(g == mx, lane, LANES), axis=1, keepdims=True)
        pick = (lane == first) & (lane < n_valid)
        sel = jnp.where(pick, 1.0, sel)
        g = jnp.where(lane == first, -jnp.inf, g)
    return sel


def _top_blocks_t(gate, n_valid, n_sel):
    blk = lax.broadcasted_iota(I32, gate.shape, 0)
    g = jnp.where(blk < n_valid, gate, -jnp.inf)
    sel = jnp.zeros(gate.shape, F32)
    for _ in range(n_sel):
        mx = jnp.max(g, axis=0, keepdims=True)
        first = jnp.min(jnp.where(g == mx, blk, LANES), axis=0, keepdims=True)
        pick = (blk == first) & (blk < n_valid)
        sel = jnp.where(pick, 1.0, sel)
        g = jnp.where(blk == first, -jnp.inf, g)
    return sel


def _moba_prompt_kernel(qt_ref, kmean_ref, k_ref, vt_ref, sg_ref, o_ref, bias_ref, *, tq, n_sel, n_units):
    i = pl.program_id(2)
    diag = _causal_bias_t(tq)
    qbd = []
    for u in range(n_units):
        qbd.append(_block_diag_qt(qt_ref[u * LANES:(u + 1) * LANES, :], tq))
        gate = jnp.dot(kmean_ref[:, u * LANES:(u + 1) * LANES].astype(BF16), qbd[u],
                       preferred_element_type=F32)
        bias_ref[u] = (_top_blocks_t(gate, i, n_sel) - 1.0) * (-NEG)

    def tile(kt, carry, diagonal):
        start = pl.multiple_of(kt * tq, tq)
        out = []
        for u in range(n_units):
            bias = diag if diagonal else bias_ref[u, pl.ds(kt, 1), :]
            s = jnp.dot(k_ref[pl.ds(start, tq), u * LANES:(u + 1) * LANES], qbd[u],
                        preferred_element_type=F32) + bias
            vts = [vt_ref[kt, u * LANES:u * LANES + HEAD_DIM, :],
                   vt_ref[kt, u * LANES + HEAD_DIM:(u + 1) * LANES, :]]
            out.append(_flash_tile_t(s, vts, *carry[u]))
        return tuple(out)

    carry = lax.fori_loop(0, i, lambda kt, c: tile(kt, c, False),
                          tuple(_flash_init_t(HEAD_DIM, tq) for _ in range(n_units)))
    carry = tile(i, carry, True)
    for u in range(n_units):
        _, l, acc = carry[u]
        o = _pair_to_rows(acc / l, tq) * sg_ref[:, u * LANES:(u + 1) * LANES]
        o_ref[:, u * LANES:(u + 1) * LANES] = o.astype(BF16)


def _moba_prompt(qt, kmean, k, vt, sg, hg=8):
    B, _, T = qt.shape
    L = k.shape[1]
    tq = MOBA_BLOCK
    nb = L // tq
    assert T == L and L % tq == 0 and nb <= LANES
    n_sel = max(1, min(MOBA_TOPB, nb - 1))
    wg = hg * HEAD_DIM
    n_units = wg // LANES
    qmap = lambda b, g, i: (b, i, g)
    kmap = lambda b, g, i: (b, 0, g)
    return pl.pallas_call(
        functools.partial(_moba_prompt_kernel, tq=tq, n_sel=n_sel, n_units=n_units),
        grid=(B, W_C // wg, T // tq),
        in_specs=[pl.BlockSpec((None, wg, tq), lambda b, g, i: (b, g, i)),
                  _resident((None, LANES, wg), kmap), _resident((None, L, wg), kmap),
                  _resident((None, nb, wg, tq), lambda b, g, i: (b, 0, g, 0)),
                  pl.BlockSpec((None, tq, wg), qmap)],
        out_specs=pl.BlockSpec((None, tq, wg), qmap),
        out_shape=jax.ShapeDtypeStruct((B, T, W_C), BF16),
        scratch_shapes=[pltpu.VMEM((n_units, LANES, 2 * tq), F32)],
        compiler_params=_cparams("parallel", "parallel", "arbitrary"),
        name="moba_prompt",
    )(qt, kmean, k, vt, sg)


def _page_map(layer, n_pages):
    def index_map(b, p, pt):
        return (pt[b, jnp.minimum(p, n_pages - 1)], layer, 0, 0)
    return index_map


def _pad_rows(a, rows):
    return jnp.concatenate([a, jnp.zeros((rows - a.shape[0], a.shape[1]), a.dtype)], axis=0)


def _new_token_bias(nq):
    r = lax.broadcasted_iota(I32, (nq, LANES), 0)
    c = lax.broadcasted_iota(I32, (nq, LANES), 1)
    return jnp.where(c <= r, 0.0, NEG).astype(F32)


def _fold_heads(o, n_heads, nq, width):
    col_head = lax.broadcasted_iota(I32, (nq, n_heads * width), 1) // width
    out = jnp.zeros((nq, n_heads * width), F32)
    for h in range(n_heads):
        out = jnp.where(col_head == h, o[h * nq:(h + 1) * nq, :], out)
    return out


def _flash_ref_update(q, k, v, bias, m_ref, l_ref, acc_ref):
    m, l, acc = _flash_tile(q, k, v, bias, m_ref[...], l_ref[...], acc_ref[...])
    m_ref[...] = m
    l_ref[...] = l
    acc_ref[...] = acc


def _flash_ref_init(m_ref, l_ref, acc_ref):
    m_ref[...] = jnp.full(m_ref.shape, NEG, F32)
    l_ref[...] = jnp.zeros(l_ref.shape, F32)
    acc_ref[...] = jnp.zeros(acc_ref.shape, F32)


def _dsa_sample_select_kernel(pt_ref, qi_ref, wi_ref, kc_ref, kn_ref, bias_ref, key_ref, thr_ref, tie_ref,
                              *, n_pages, nq, topk, col_bits):
    p = pl.program_id(1)
    qi = qi_ref[...]
    wi = wi_ref[...]

    def score(kblk):
        rel = lax.dot_general(qi, kblk, (((1,), (1,)), ((), ())), preferred_element_type=F32)
        wr = wi * jnp.maximum(rel, 0.0)
        s = jnp.zeros((nq, kblk.shape[0]), F32)
        for h in range(H_IDX):
            s = s + wr[h * nq:(h + 1) * nq, :]
        return s

    @pl.when(p < n_pages)
    def _():
        key_ref[p] = _sortable_key(score(kc_ref[...].astype(BF16)))

    @pl.when(p == n_pages)
    def _():
        s = score(_pad_rows(kn_ref[...], LANES).astype(BF16))
        r = lax.broadcasted_iota(I32, (nq, LANES), 0)
        lane = lax.broadcasted_iota(I32, (nq, LANES), 1)
        key_ref[n_pages] = jnp.where(lane <= r, _sortable_key(s), INT_MIN)
        _topk_select(key_ref, thr_ref, tie_ref, n_pages + 1, nq, topk, col_bits, rb=nq)
        thr = thr_ref[...]
        tie = tie_ref[...]

        def bias_chunk(c, carry):
            bias_ref[c] = _select_bias(key_ref[c], thr, tie, c * LANES + lane)
            return carry

        lax.fori_loop(0, n_pages + 1, bias_chunk, 0)


def _dsa_sample_select(page_table, layer, qi_s, wi_s, cache_idx, ki_new):
    Bd, n_pages = page_table.shape
    nq = ki_new.shape[1]
    L = n_pages * cache_idx.shape[2] + nq
    topk = min(TOPK_MAX, L // 4)
    assert cache_idx.shape[2] == LANES and nq == 8
    bmap = lambda b, p, pt: (b, 0, 0)
    return pl.pallas_call(
        functools.partial(_dsa_sample_select_kernel, n_pages=n_pages, nq=nq, topk=topk,
                          col_bits=max(1, (L - 1).bit_length())),
        grid_spec=pltpu.PrefetchScalarGridSpec(
            num_scalar_prefetch=1, grid=(Bd, n_pages + 1),
            in_specs=[pl.BlockSpec((None,) + qi_s.shape[1:], bmap), pl.BlockSpec((None,) + wi_s.shape[1:], bmap),
                      pl.BlockSpec((None, None, LANES, D_IDX), _page_map(layer, n_pages)),
                      pl.BlockSpec((None, nq, D_IDX), bmap)],
            out_specs=pl.BlockSpec((None, n_pages + 1, nq, LANES), lambda b, p, pt: (b, 0, 0, 0)),
            scratch_shapes=[pltpu.VMEM((n_pages + 1, nq, LANES), I32), pltpu.VMEM((nq, LANES), I32),
                            pltpu.VMEM((nq, LANES), I32)]),
        out_shape=jax.ShapeDtypeStruct((Bd, n_pages + 1, nq, LANES), F32),
        compiler_params=_cparams("parallel", "arbitrary"),
        name="dsa_sample_select",
    )(page_table, qi_s, wi_s, cache_idx, ki_new)


def _dsa_sample_attn_kernel(pt_ref, q_ref, kvc_ref, kvn_ref, bias_ref, sg_ref, o_ref, m_ref, l_ref, acc_ref,
                            *, n_pages, nq):
    p = pl.program_id(1)
    q = q_ref[...]
    bias = jnp.concatenate([bias_ref[...]] * H_A, axis=0)

    @pl.when(p == 0)
    def _():
        _flash_ref_init(m_ref, l_ref, acc_ref)

    @pl.when(p < n_pages)
    def _():
        _flash_ref_update(q, kvc_ref[:, 0:W_A].astype(BF16), kvc_ref[:, W_A:2 * W_A].astype(BF16), bias,
                          m_ref, l_ref, acc_ref)

    @pl.when(p == n_pages)
    def _():
        kv = _pad_rows(kvn_ref[...], LANES).astype(BF16)
        _flash_ref_update(q, kv[:, 0:W_A], kv[:, W_A:2 * W_A], bias, m_ref, l_ref, acc_ref)
        o = _fold_heads(acc_ref[...] / l_ref[...], H_A, nq, HEAD_DIM)
        o_ref[...] = (o * sg_ref[...]).astype(BF16)


def _dsa_sample_attn(page_table, layer, q_bd, cache_kv, kv_new, bias, sg):
    Bd, n_pages = page_table.shape
    nq = kv_new.shape[1]
    rows = q_bd.shape[1]
    bmap = lambda b, p, pt: (b, 0, 0)
    return pl.pallas_call(
        functools.partial(_dsa_sample_attn_kernel, n_pages=n_pages, nq=nq),
        grid_spec=pltpu.PrefetchScalarGridSpec(
            num_scalar_prefetch=1, grid=(Bd, n_pages + 1),
            in_specs=[pl.BlockSpec((None, rows, W_A), bmap),
                      pl.BlockSpec((None, None, LANES, 2 * W_A), _page_map(layer, n_pages)),
                      pl.BlockSpec((None, nq, 2 * W_A), bmap),
                      pl.BlockSpec((None, None, nq, LANES), lambda b, p, pt: (b, p, 0, 0)),
                      pl.BlockSpec((None, nq, W_A), bmap)],
            out_specs=pl.BlockSpec((None, nq, W_A), bmap),
            scratch_shapes=[pltpu.VMEM((rows, 1), F32), pltpu.VMEM((rows, 1), F32), pltpu.VMEM((rows, W_A), F32)]),
        out_shape=jax.ShapeDtypeStruct((Bd, nq, W_A), BF16),
        compiler_params=_cparams("parallel", "arbitrary"),
        name="dsa_sample_attn",
    )(page_table, q_bd, cache_kv, kv_new, bias, sg)


def _diff_sample_kernel(pt_ref, q_ref, kvc_ref, kvn_ref, sg_ref, lam_ref, subg_ref, o_ref, m_ref, l_ref, acc_ref,
                        *, n_pages, nq, lam_init):
    p = pl.program_id(1)
    q = q_ref[...]

    @pl.when(p == 0)
    def _():
        _flash_ref_init(m_ref, l_ref, acc_ref)

    @pl.when(p < n_pages)
    def _():
        _flash_ref_update(q, kvc_ref[:, 0:W_B].astype(BF16), kvc_ref[:, W_B:2 * W_B].astype(BF16), None,
                          m_ref, l_ref, acc_ref)

    @pl.when(p == n_pages)
    def _():
        kv = _pad_rows(kvn_ref[...], LANES).astype(BF16)
        bias = jnp.concatenate([_new_token_bias(nq)] * (2 * H_B), axis=0)
        _flash_ref_update(q, kv[:, 0:W_B], kv[:, W_B:2 * W_B], bias, m_ref, l_ref, acc_ref)
        lam_val = _lambda_value(lam_ref, lam_init)
        dv = 2 * DH_B
        l = l_ref[...]
        for h in range(H_B):
            r1, r2 = (2 * h) * nq, (2 * h + 1) * nq
            o = _diff_finish(acc_ref[r1:r1 + nq, h * dv:(h + 1) * dv], l[r1:r1 + nq],
                             acc_ref[r2:r2 + nq, h * dv:(h + 1) * dv], l[r2:r2 + nq],
                             lam_val, subg_ref[...], sg_ref[:, h * dv:(h + 1) * dv], lam_init)
            o_ref[:, h * dv:(h + 1) * dv] = o.astype(BF16)


def _diff_sample(page_table, layer, q_bd, cache_kv, kv_new, sg, lam, subg, lam_init):
    Bd, n_pages = page_table.shape
    nq = kv_new.shape[1]
    rows = q_bd.shape[1]
    bmap = lambda b, p, pt: (b, 0, 0)
    const = lambda b, p, pt: (0, 0)
    return pl.pallas_call(
        functools.partial(_diff_sample_kernel, n_pages=n_pages, nq=nq, lam_init=lam_init),
        grid_spec=pltpu.PrefetchScalarGridSpec(
            num_scalar_prefetch=1, grid=(Bd, n_pages + 1),
            in_specs=[pl.BlockSpec((None, rows, W_B), bmap),
                      pl.BlockSpec((None, None, LANES, 2 * W_B), _page_map(layer, n_pages)),
                      pl.BlockSpec((None, nq, 2 * W_B), bmap),
                      pl.BlockSpec((None, nq, W_B), bmap),
                      pl.BlockSpec(lam.shape, const), pl.BlockSpec(subg.shape, const)],
            out_specs=pl.BlockSpec((None, nq, W_B), bmap),
            scratch_shapes=[pltpu.VMEM((rows, 1), F32), pltpu.VMEM((rows, 1), F32), pltpu.VMEM((rows, W_B), F32)]),
        out_shape=jax.ShapeDtypeStruct((Bd, nq, W_B), BF16),
        compiler_params=_cparams("parallel", "arbitrary"),
        name="diff_sample",
    )(page_table, q_bd, cache_kv, kv_new, sg, lam, subg)


def _moba_sample_gate_kernel(pt_ref, q_ref, kc_ref, sel_ref, ksum_ref, gate_ref, *, n_pages, ppb, n_sel):
    p = pl.program_id(1)
    colsum = jnp.sum(kc_ref[...], axis=0, keepdims=True)

    @pl.when(p % ppb == 0)
    def _():
        ksum_ref[...] = colsum

    @pl.when(p % ppb != 0)
    def _():
        ksum_ref[...] = ksum_ref[...] + colsum

    @pl.when(p == 0)
    def _():
        gate_ref[...] = jnp.zeros(gate_ref.shape, F32)

    @pl.when(p % ppb == ppb - 1)
    def _():
        mean = (ksum_ref[...] * (1.0 / MOBA_BLOCK)).astype(BF16)
        g = lax.dot_general(q_ref[...], jnp.broadcast_to(mean, (8, mean.shape[1])), (((1,), (1,)), ((), ())),
                            preferred_element_type=F32)
        lane = lax.broadcasted_iota(I32, gate_ref.shape, 1)
        gate_ref[...] = jnp.where(lane == p // ppb, g[:, 0:1], gate_ref[...])

    @pl.when(p == n_pages - 1)
    def _():
        sel_ref[...] = _top_blocks(gate_ref[...], n_pages // ppb, n_sel)


def _moba_sample_gate(page_table, layer, q_bd, cache_kv):
    Bd, n_pages = page_table.shape
    rows = q_bd.shape[1]
    ppb = MOBA_BLOCK // LANES
    nbp = n_pages // ppb
    assert n_pages % ppb == 0 and nbp <= LANES
    n_sel = max(1, min(MOBA_TOPB, nbp))
    bmap = lambda b, p, pt: (b, 0, 0)
    return pl.pallas_call(
        functools.partial(_moba_sample_gate_kernel, n_pages=n_pages, ppb=ppb, n_sel=n_sel),
        grid_spec=pltpu.PrefetchScalarGridSpec(
            num_scalar_prefetch=1, grid=(Bd, n_pages),
            in_specs=[pl.BlockSpec((None, rows, W_C), bmap),
                      pl.BlockSpec((None, None, LANES, W_C), _page_map(layer, n_pages))],
            out_specs=pl.BlockSpec((None, rows, LANES), bmap),
            scratch_shapes=[pltpu.VMEM((1, W_C), F32), pltpu.VMEM((rows, LANES), F32)]),
        out_shape=jax.ShapeDtypeStruct((Bd, rows, LANES), F32),
        compiler_params=_cparams("parallel", "arbitrary"),
        name="moba_sample_gate",
    )(page_table, q_bd, cache_kv)


def _moba_sample_attn_kernel(pt_ref, q_ref, kvc_ref, kvn_ref, sel_ref, sg_ref, o_ref, m_ref, l_ref, acc_ref,
                             *, n_pages, nq, ppb):
    p = pl.program_id(1)
    q = q_ref[...]

    @pl.when(p == 0)
    def _():
        _flash_ref_init(m_ref, l_ref, acc_ref)

    @pl.when(p < n_pages)
    def _():
        blk_row = lax.broadcasted_iota(I32, (LANES, LANES), 0)
        onehot = jnp.where(blk_row == p // ppb, 1.0, 0.0).astype(BF16)
        picked = jnp.dot(sel_ref[...].astype(BF16), onehot, preferred_element_type=F32)
        bias = (picked - 1.0) * (-NEG)
        _flash_ref_update(q, kvc_ref[:, 0:W_C].astype(BF16), kvc_ref[:, W_C:2 * W_C].astype(BF16), bias,
                          m_ref, l_ref, acc_ref)

    @pl.when(p == n_pages)
    def _():
        kv = _pad_rows(kvn_ref[...], LANES).astype(BF16)
        bias = jnp.concatenate([_new_token_bias(nq)] * H_C, axis=0)
        _flash_ref_update(q, kv[:, 0:W_C], kv[:, W_C:2 * W_C], bias, m_ref, l_ref, acc_ref)
        o = _fold_heads(acc_ref[...] / l_ref[...], H_C, nq, HEAD_DIM)
        o_ref[...] = (o * sg_ref[...]).astype(BF16)


def _moba_sample_attn(page_table, layer, q_bd, cache_kv, kv_new, sel, sg):
    Bd, n_pages = page_table.shape
    nq = kv_new.shape[1]
    rows = q_bd.shape[1]
    bmap = lambda b, p, pt: (b, 0, 0)
    return pl.pallas_call(
        functools.partial(_moba_sample_attn_kernel, n_pages=n_pages, nq=nq, ppb=MOBA_BLOCK // LANES),
        grid_spec=pltpu.PrefetchScalarGridSpec(
            num_scalar_prefetch=1, grid=(Bd, n_pages + 1),
            in_specs=[pl.BlockSpec((None, rows, W_C), bmap),
                      pl.BlockSpec((None, None, LANES, 2 * W_C), _page_map(layer, n_pages)),
                      pl.BlockSpec((None, nq, 2 * W_C), bmap),
                      pl.BlockSpec((None, rows, LANES), bmap),
                      pl.BlockSpec((None, nq, W_C), bmap)],
            out_specs=pl.BlockSpec((None, nq, W_C), bmap),
            scratch_shapes=[pltpu.VMEM((rows, 1), F32), pltpu.VMEM((rows, 1), F32), pltpu.VMEM((rows, W_C), F32)]),
        out_shape=jax.ShapeDtypeStruct((Bd, nq, W_C), BF16),
        compiler_params=_cparams("parallel", "arbitrary"),
        name="moba_sample_attn",
    )(page_table, q_bd, cache_kv, kv_new, sel, sg)


def _pages_per_step(n_pages, cap):
    pps = cap
    while n_pages % pps:
        pps //= 2
    return pps


def _page_specs(block, layer, pps):
    def spec(j):
        def index_map(b, s, pt):
            return (pt[b, s * pps + j], layer) + (0,) * (len(block) - 2)
        return pl.BlockSpec(block, index_map)
    return [spec(j) for j in range(pps)]


def _head_match_bias(n_heads, nq, n_slots):
    r = lax.broadcasted_iota(I32, (n_heads * nq, n_slots * n_heads), 0)
    c = lax.broadcasted_iota(I32, (n_heads * nq, n_slots * n_heads), 1)
    return jnp.where(c % n_heads == r // nq, 0.0, NEG).astype(F32)


def _new_token_flat_bias(n_heads, nq):
    r = lax.broadcasted_iota(I32, (n_heads * nq, nq * n_heads), 0)
    c = lax.broadcasted_iota(I32, (n_heads * nq, nq * n_heads), 1)
    ok = (c % n_heads == r // nq) & (c // n_heads <= r % nq)
    return jnp.where(ok, 0.0, NEG).astype(F32)


def _flash_rows(q, k, v, bias, idx, m_ref, l_ref, acc_ref):
    m, l, acc = _flash_tile(q, k, v, bias, m_ref[idx], l_ref[idx], acc_ref[idx])
    m_ref[idx] = m
    l_ref[idx] = l
    acc_ref[idx] = acc


def _rows_to_heads(o, n_heads, nq):
    return jnp.concatenate([o[h * nq:(h + 1) * nq, :] for h in range(n_heads)], axis=1)


def _paged_step(q, ks, vs, bias, m_ref, l_ref, acc_ref, transposed):
    nt = (((1,), (1,)), ((), ()))
    if transposed:
        s = [jnp.dot(q, k, preferred_element_type=F32) for k in ks]
    else:
        s = [lax.dot_general(q, k, nt, preferred_element_type=F32) for k in ks]
    s = jnp.concatenate(s, axis=1) if len(s) > 1 else s[0]
    if bias is not None:
        s = s + bias
    m_prev = m_ref[...]
    m_new = jnp.maximum(m_prev, jnp.max(s, axis=1, keepdims=True))
    alpha = jnp.exp(m_prev - m_new)
    p = jnp.exp(s - m_new)
    l_ref[...] = alpha * l_ref[...] + jnp.sum(p, axis=1, keepdims=True)
    pb = p.astype(BF16)
    pv = None
    for j, v in enumerate(vs):
        pj = pb[:, j * LANES:(j + 1) * LANES] if len(vs) > 1 else pb
        t = (lax.dot_general(pj, v, nt, preferred_element_type=F32) if transposed
             else jnp.dot(pj, v, preferred_element_type=F32))
        pv = t if pv is None else pv + t
    acc_ref[...] = alpha * acc_ref[...] + pv
    m_ref[...] = m_new


def _dsa_select_paged_kernel(pt_ref, qi_ref, wi_ref, *refs, n_pages, pps, nq, topk, col_bits):
    kc_refs = refs[:pps]
    kn_ref, bias_ref, key_ref, thr_ref, tie_ref = refs[pps:]
    s = pl.program_id(1)
    qi = qi_ref[...]
    wi = wi_ref[...]

    def score(rel):
        wr = wi * jnp.maximum(rel, 0.0)
        out = jnp.zeros((nq, rel.shape[1]), F32)
        for h in range(H_IDX):
            out = out + wr[h * nq:(h + 1) * nq, :]
        return out

    for j in range(pps):
        rel = jnp.dot(qi, kc_refs[j][...].astype(BF16), preferred_element_type=F32)
        key_ref[s * pps + j] = _sortable_key(score(rel))

    @pl.when(s == n_pages // pps - 1)
    def _():
        kn = _pad_rows(kn_ref[...], LANES).astype(BF16)
        sc = score(lax.dot_general(qi, kn, (((1,), (1,)), ((), ())), preferred_element_type=F32))
        r = lax.broadcasted_iota(I32, (nq, LANES), 0)
        lane = lax.broadcasted_iota(I32, (nq, LANES), 1)
        key_ref[n_pages] = jnp.where(lane <= r, _sortable_key(sc), INT_MIN)
        _topk_select(key_ref, thr_ref, tie_ref, n_pages + 1, nq, topk, col_bits, rb=nq)
        thr = thr_ref[...]
        tie = tie_ref[...]

        def bias_chunk(c, carry):
            bias_ref[c] = _select_bias(key_ref[c], thr, tie, c * LANES + lane)
            return carry

        lax.fori_loop(0, n_pages + 1, bias_chunk, 0)


def _dsa_select_paged(page_table, layer, qi_s, wi_s, cache_idx_t, ki_new):
    Bd, n_pages = page_table.shape
    nq = ki_new.shape[1]
    L = n_pages * LANES + nq
    topk = min(TOPK_MAX, L // 4)
    pps = _pages_per_step(n_pages, 8)
    assert cache_idx_t.shape[3] == LANES and nq == 8
    bmap = lambda b, s, pt: (b, 0, 0)
    return pl.pallas_call(
        functools.partial(_dsa_select_paged_kernel, n_pages=n_pages, pps=pps, nq=nq, topk=topk,
                          col_bits=max(1, (L - 1).bit_length())),
        grid_spec=pltpu.PrefetchScalarGridSpec(
            num_scalar_prefetch=1, grid=(Bd, n_pages // pps),
            in_specs=[pl.BlockSpec((None,) + qi_s.shape[1:], bmap), pl.BlockSpec((None,) + wi_s.shape[1:], bmap)]
                     + _page_specs((None, None, D_IDX, LANES), layer, pps)
                     + [pl.BlockSpec((None, nq, D_IDX), bmap)],
            out_specs=pl.BlockSpec((None, n_pages + 1, nq, LANES), lambda b, s, pt: (b, 0, 0, 0)),
            scratch_shapes=[pltpu.VMEM((n_pages + 1, nq, LANES), I32), pltpu.VMEM((nq, LANES), I32),
                            pltpu.VMEM((nq, LANES), I32)]),
        out_shape=jax.ShapeDtypeStruct((Bd, n_pages + 1, nq, LANES), F32),
        compiler_params=_cparams("parallel", "arbitrary"),
        name="dsa_sample_select",
    )(page_table, qi_s, wi_s, *([cache_idx_t] * pps), ki_new)


def _dsa_attn_paged_kernel(pt_ref, q_ref, bias_ref, biasn_ref, *refs, n_pages, pps, nq):
    kv_refs = refs[:pps]
    kvn_ref, sg_ref, o_ref, m_ref, l_ref, acc_ref = refs[pps:]
    s = pl.program_id(1)
    q = q_ref[...]

    @pl.when(s == 0)
    def _():
        _flash_ref_init(m_ref, l_ref, acc_ref)

    ks = [kv_refs[j][0].reshape(W_A, LANES).astype(BF16) for j in range(pps)]
    vs = [kv_refs[j][1].reshape(W_A, LANES).astype(BF16) for j in range(pps)]
    bias = jnp.concatenate([jnp.concatenate([bias_ref[j]] * H_A, axis=0) for j in range(pps)], axis=1)
    _paged_step(q, ks, vs, bias, m_ref, l_ref, acc_ref, transposed=True)

    @pl.when(s == n_pages // pps - 1)
    def _():
        kv = _pad_rows(kvn_ref[...], LANES).astype(BF16)
        bias_n = jnp.concatenate([biasn_ref[...]] * H_A, axis=0)
        _paged_step(q, [kv[:, 0:W_A]], [kv[:, W_A:2 * W_A]], bias_n, m_ref, l_ref, acc_ref, transposed=False)
        o = _fold_heads(acc_ref[...] / l_ref[...], H_A, nq, HEAD_DIM)
        o_ref[...] = (o * sg_ref[...]).astype(BF16)


def _dsa_attn_paged(page_table, layer, q_bd, cache_kv_t, kv_new, bias, sg):
    Bd, n_pages = page_table.shape
    nq = kv_new.shape[1]
    rows = q_bd.shape[1]
    pps = _pages_per_step(n_pages, 8)
    bmap = lambda b, s, pt: (b, 0, 0)
    return pl.pallas_call(
        functools.partial(_dsa_attn_paged_kernel, n_pages=n_pages, pps=pps, nq=nq),
        grid_spec=pltpu.PrefetchScalarGridSpec(
            num_scalar_prefetch=1, grid=(Bd, n_pages // pps),
            in_specs=[pl.BlockSpec((None, rows, W_A), bmap),
                      pl.BlockSpec((None, pps, nq, LANES), lambda b, s, pt: (b, s, 0, 0)),
                      pl.BlockSpec((None, None, nq, LANES), lambda b, s, pt: (b, n_pages, 0, 0))]
                     + _page_specs((None, None, 2, H_A, HEAD_DIM, LANES), layer, pps)
                     + [pl.BlockSpec((None, nq, 2 * W_A), bmap), pl.BlockSpec((None, nq, W_A), bmap)],
            out_specs=pl.BlockSpec((None, nq, W_A), bmap),
            scratch_shapes=[pltpu.VMEM((rows, 1), F32), pltpu.VMEM((rows, 1), F32), pltpu.VMEM((rows, W_A), F32)]),
        out_shape=jax.ShapeDtypeStruct((Bd, nq, W_A), BF16),
        compiler_params=_cparams("parallel", "arbitrary"),
        name="dsa_sample_attn",
    )(page_table, q_bd, bias, bias, *([cache_kv_t] * pps), kv_new, sg)


def _diff_paged_kernel(pt_ref, q_ref, *refs, n_pages, pps, nq, lam_init):
    kv_refs = refs[:pps]
    kvn_ref, sg_ref, lam_ref, subg_ref, o_ref, m_ref, l_ref, acc_ref = refs[pps:]
    s = pl.program_id(1)
    q = q_ref[...]

    @pl.when(s == 0)
    def _():
        _flash_ref_init(m_ref, l_ref, acc_ref)

    ks = [kv_refs[j][:, 0:W_B].astype(BF16) for j in range(pps)]
    vs = [kv_refs[j][:, W_B:2 * W_B].astype(BF16) for j in range(pps)]
    _paged_step(q, ks, vs, None, m_ref, l_ref, acc_ref, transposed=False)

    @pl.when(s == n_pages // pps - 1)
    def _():
        kv = _pad_rows(kvn_ref[...], LANES).astype(BF16)
        bias = jnp.concatenate([_new_token_bias(nq)] * (2 * H_B), axis=0)
        _paged_step(q, [kv[:, 0:W_B]], [kv[:, W_B:2 * W_B]], bias, m_ref, l_ref, acc_ref, transposed=False)
        lam_val = _lambda_value(lam_ref, lam_init)
        dv = 2 * DH_B
        l = l_ref[...]
        for h in range(H_B):
            r1, r2 = (2 * h) * nq, (2 * h + 1) * nq
            o = _diff_finish(acc_ref[r1:r1 + nq, h * dv:(h + 1) * dv], l[r1:r1 + nq],
                             acc_ref[r2:r2 + nq, h * dv:(h + 1) * dv], l[r2:r2 + nq],
                             lam_val, subg_ref[...], sg_ref[:, h * dv:(h + 1) * dv], lam_init)
            o_ref[:, h * dv:(h + 1) * dv] = o.astype(BF16)


def _diff_paged(page_table, layer, q_bd, cache_kv, kv_new, sg, lam, subg, lam_init):
    Bd, n_pages = page_table.shape
    nq = kv_new.shape[1]
    rows = q_bd.shape[1]
    pps = _pages_per_step(n_pages, 8)
    bmap = lambda b, s, pt: (b, 0, 0)
    const = lambda b, s, pt: (0, 0)
    return pl.pallas_call(
        functools.partial(_diff_paged_kernel, n_pages=n_pages, pps=pps, nq=nq, lam_init=lam_init),
        grid_spec=pltpu.PrefetchScalarGridSpec(
            num_scalar_prefetch=1, grid=(Bd, n_pages // pps),
            in_specs=[pl.BlockSpec((None, rows, W_B), bmap)]
                     + _page_specs((None, None, LANES, 2 * W_B), layer, pps)
                     + [pl.BlockSpec((None, nq, 2 * W_B), bmap), pl.BlockSpec((None, nq, W_B), bmap),
                        pl.BlockSpec(lam.shape, const), pl.BlockSpec(subg.shape, const)],
            out_specs=pl.BlockSpec((None, nq, W_B), bmap),
            scratch_shapes=[pltpu.VMEM((rows, 1), F32), pltpu.VMEM((rows, 1), F32), pltpu.VMEM((rows, W_B), F32)]),
        out_shape=jax.ShapeDtypeStruct((Bd, nq, W_B), BF16),
        compiler_params=_cparams("parallel", "arbitrary"),
        name="diff_sample",
    )(page_table, q_bd, *([cache_kv] * pps), kv_new, sg, lam, subg)


def _moba_gate_paged_kernel(pt_ref, q_ref, *refs, n_pages, pps, ppb, n_sel):
    kc_refs = refs[:pps]
    sel_ref, gate_ref = refs[pps:]
    s = pl.program_id(1)
    q = q_ref[...]

    @pl.when(s == 0)
    def _():
        gate_ref[...] = jnp.zeros(gate_ref.shape, F32)

    lane = lax.broadcasted_iota(I32, gate_ref.shape, 1)
    for bi in range(pps // ppb):
        tot = None
        for j in range(bi * ppb, (bi + 1) * ppb):
            sc = jnp.dot(q, kc_refs[j][...].reshape(W_C, LANES).astype(BF16), preferred_element_type=F32)
            rs = jnp.sum(sc, axis=1, keepdims=True)
            tot = rs if tot is None else tot + rs
        gate_ref[...] = jnp.where(lane == s * (pps // ppb) + bi, tot * (1.0 / MOBA_BLOCK), gate_ref[...])

    @pl.when(s == n_pages // pps - 1)
    def _():
        sel_ref[...] = _top_blocks(gate_ref[...], n_pages // ppb, n_sel)


def _moba_gate_paged(page_table, layer, q_bd, cache_kv_t):
    Bd, n_pages = page_table.shape
    rows = q_bd.shape[1]
    ppb = MOBA_BLOCK // LANES
    nbp = n_pages // ppb
    pps = _pages_per_step(n_pages, 8)
    assert n_pages % ppb == 0 and pps % ppb == 0 and nbp <= LANES
    n_sel = max(1, min(MOBA_TOPB, nbp))
    bmap = lambda b, s, pt: (b, 0, 0)

    def kspec(j):
        def index_map(b, s, pt):
            return (pt[b, s * pps + j], layer, 0, 0, 0, 0)
        return pl.BlockSpec((None, None, None, H_C, HEAD_DIM, LANES), index_map)

    return pl.pallas_call(
        functools.partial(_moba_gate_paged_kernel, n_pages=n_pages, pps=pps, ppb=ppb, n_sel=n_sel),
        grid_spec=pltpu.PrefetchScalarGridSpec(
            num_scalar_prefetch=1, grid=(Bd, n_pages // pps),
            in_specs=[pl.BlockSpec((None, rows, W_C), bmap)] + [kspec(j) for j in range(pps)],
            out_specs=pl.BlockSpec((None, rows, LANES), bmap),
            scratch_shapes=[pltpu.VMEM((rows, LANES), F32)]),
        out_shape=jax.ShapeDtypeStruct((Bd, rows, LANES), F32),
        compiler_params=_cparams("parallel", "arbitrary"),
        name="moba_sample_gate",
    )(page_table, q_bd, *([cache_kv_t] * pps))


def _moba_attn_paged_kernel(pt_ref, q_ref, sel_ref, *refs, n_pages, pps, ppb, nq):
    kv_refs = refs[:pps]
    kvn_ref, sg_ref, o_ref, m_ref, l_ref, acc_ref = refs[pps:]
    s = pl.program_id(1)
    q = q_ref[...]
    blk_row = lax.broadcasted_iota(I32, (LANES, LANES), 0)

    @pl.when(s == 0)
    def _():
        _flash_ref_init(m_ref, l_ref, acc_ref)

    sel = sel_ref[...].astype(BF16)
    bias = []
    for j in range(pps):
        onehot = jnp.where(blk_row == (s * pps + j) // ppb, 1.0, 0.0).astype(BF16)
        picked = jnp.dot(sel, onehot, preferred_element_type=F32)
        bias.append((picked - 1.0) * (-NEG))
    ks = [kv_refs[j][0].reshape(W_C, LANES).astype(BF16) for j in range(pps)]
    vs = [kv_refs[j][1].reshape(W_C, LANES).astype(BF16) for j in range(pps)]
    _paged_step(q, ks, vs, jnp.concatenate(bias, axis=1), m_ref, l_ref, acc_ref, transposed=True)

    @pl.when(s == n_pages // pps - 1)
    def _():
        kv = _pad_rows(kvn_ref[...], LANES).astype(BF16)
        bias_n = jnp.concatenate([_new_token_bias(nq)] * H_C, axis=0)
        _paged_step(q, [kv[:, 0:W_C]], [kv[:, W_C:2 * W_C]], bias_n, m_ref, l_ref, acc_ref, transposed=False)
        o = _fold_heads(acc_ref[...] / l_ref[...], H_C, nq, HEAD_DIM)
        o_ref[...] = (o * sg_ref[...]).astype(BF16)


def _moba_attn_paged(page_table, layer, q_bd, cache_kv_t, kv_new, sel, sg):
    Bd, n_pages = page_table.shape
    nq = kv_new.shape[1]
    rows = q_bd.shape[1]
    pps = _pages_per_step(n_pages, 8)
    bmap = lambda b, s, pt: (b, 0, 0)
    return pl.pallas_call(
        functools.partial(_moba_attn_paged_kernel, n_pages=n_pages, pps=pps, ppb=MOBA_BLOCK // LANES, nq=nq),
        grid_spec=pltpu.PrefetchScalarGridSpec(
            num_scalar_prefetch=1, grid=(Bd, n_pages // pps),
            in_specs=[pl.BlockSpec((None, rows, W_C), bmap), pl.BlockSpec((None, rows, LANES), bmap)]
                     + _page_specs((None, None, 2, H_C, HEAD_DIM, LANES), layer, pps)
                     + [pl.BlockSpec((None, nq, 2 * W_C), bmap), pl.BlockSpec((None, nq, W_C), bmap)],
            out_specs=pl.BlockSpec((None, nq, W_C), bmap),
            scratch_shapes=[pltpu.VMEM((rows, 1), F32), pltpu.VMEM((rows, 1), F32), pltpu.VMEM((rows, W_C), F32)]),
        out_shape=jax.ShapeDtypeStruct((Bd, nq, W_C), BF16),
        compiler_params=_cparams("parallel", "arbitrary"),
        name="moba_sample_attn",
    )(page_table, q_bd, sel, *([cache_kv_t] * pps), kv_new, sg)


def _block_diag_queries(q2, Bd, nq, n_heads, width):
    q4 = q2.reshape(Bd, nq, n_heads, width)
    eye = jnp.eye(n_heads, dtype=q2.dtype)
    return jnp.einsum('bqhd,hg->bhqgd', q4, eye).reshape(Bd, n_heads * nq, n_heads * width)


def _sample_trunk(x, past_len, caches, page_table, norm_g, w_even, w_out_even, lam_even, subln_g_even,
                  w_odd, w_out_odd, final_norm_g):
    cache_a_kv, cache_a_idx, cache_b_kv, cache_c_kv = caches
    Bd, nq, D = x.shape
    M = Bd * nq
    n_pool, _, page, _ = cache_a_idx.shape
    assert page == LANES and past_len % MOBA_BLOCK == 0 and nq <= 8
    cb_kv = cache_b_kv.reshape(n_pool, cache_b_kv.shape[1], page, 2 * W_B)
    ca_kv_t = cache_a_kv.transpose(0, 1, 3, 4, 5, 2)
    ca_idx_t = cache_a_idx.transpose(0, 1, 3, 2)
    cc_kv_t = cache_c_kv.transpose(0, 1, 3, 4, 5, 2)
    depth = norm_g.shape[0]
    heads_first = lambda a, nh: a.reshape(Bd, nq, nh, HEAD_DIM).transpose(0, 2, 1, 3).reshape(Bd, nh * nq, HEAD_DIM)
    cos, sin = _rope_tables(past_len + jnp.arange(nq, dtype=I32))
    cos, sin = jnp.tile(cos, (Bd, 1)), jnp.tile(sin, (Bd, 1))
    x2 = x.reshape(M, D)
    a_kv, a_idx, b_kv, c_kv = [], [], [], []
    y = None
    r3 = lambda a: a.reshape(Bd, nq, a.shape[1])
    for li in range(depth):
        j = li // 2
        g = norm_g[li][None, :]
        last = li == depth - 1
        gf = final_norm_g[None, :] if last else None
        if li % 2 == 0:
            lam_init = 0.8 - 0.6 * math.exp(-0.3 * li)
            (qa, kva, _, _, sg, qi, kif, _, wi, qb, kvb, _, _) = _even_proj(x2, g, *w_even[j], cos, sin, M)
            wi_s = wi.reshape(Bd, nq, H_IDX).transpose(0, 2, 1).reshape(Bd, H_IDX * nq, 1)
            bias = _dsa_select_paged(page_table, j, heads_first(qi, H_IDX), wi_s, ca_idx_t, r3(kif))
            oa = _dsa_attn_paged(page_table, j, _block_diag_queries(qa, Bd, nq, H_A, HEAD_DIM), ca_kv_t,
                                 r3(kva), bias, r3(sg[:, :W_A]))
            ob = _diff_paged(page_table, j, _block_diag_queries(qb, Bd, nq, 2 * H_B, DH_B), cb_kv, r3(kvb),
                             r3(sg[:, W_A:]), lam_even[j], subln_g_even[j][None, :], lam_init)
            res = _out_proj([oa.reshape(M, W_A), ob.reshape(M, W_B)], w_out_even[j], x2, gf, M)
            a_kv.append(kva.reshape(Bd, nq, 2, H_A, HEAD_DIM))
            a_idx.append(kif.reshape(Bd, nq, D_IDX))
            b_kv.append(kvb.reshape(Bd, nq, 2, H_B, 2 * DH_B))
        else:
            q, kv, _, _, sg, _ = _odd_proj(x2, g, w_odd[j], cos, sin, M)
            q_bd = _block_diag_queries(q, Bd, nq, H_C, HEAD_DIM)
            sel = _moba_gate_paged(page_table, j, q_bd, cc_kv_t)
            o = _moba_attn_paged(page_table, j, q_bd, cc_kv_t, r3(kv), sel, r3(sg))
            res = _out_proj([o.reshape(M, W_C)], w_out_odd[j], x2, gf, M)
            c_kv.append(kv.reshape(Bd, nq, 2, H_C, HEAD_DIM))
        x2 = res[0]
        if last:
            y = res[1]
    return (y.reshape(Bd, nq, D), jnp.stack(a_kv, axis=1), jnp.stack(a_idx, axis=1),
            jnp.stack(b_kv, axis=1), jnp.stack(c_kv, axis=1))


def _rope_tables(pos):
    half = HEAD_DIM // 2
    inv_freq = jnp.exp(-math.log(ROPE_THETA) * jnp.arange(half, dtype=F32) / half)
    ang = pos.astype(F32)[:, None] * inv_freq[None, :]
    cos, sin = jnp.cos(ang), jnp.sin(ang)
    return jnp.concatenate([cos, cos, cos, cos], axis=1), jnp.concatenate([-sin, sin, -sin, sin], axis=1)


def _split_even_weight(w):
    wa = w[:, 0:4 * W_A].astype(BF16)
    wi = jnp.pad(w[:, 4 * W_A:4 * W_A + IDX_COLS], ((0, 0), (0, IDX_PAD - IDX_COLS))).astype(BF16)
    wb = w[:, 4 * W_A + IDX_COLS:].astype(BF16)
    return wa, wi, wb


def _prompt_trunk(x, norm_g, w_even, w_out_even, lam_even, subln_g_even, w_odd, w_out_odd, final_norm_g):
    B, T, D = x.shape
    M = B * T
    tm = min(256, T)
    depth = norm_g.shape[0]
    cos, sin = _rope_tables(jnp.arange(T, dtype=I32))
    cos, sin = jnp.tile(cos, (B, 1)), jnp.tile(sin, (B, 1))
    x2 = x.reshape(M, D)
    a_kv, a_idx, b_kv, c_kv = [], [], [], []
    y = None
    r3 = lambda a: a.reshape(B, T, a.shape[1])
    tr = lambda a: r3(a).transpose(0, 2, 1)
    tile_tr = lambda a: a.reshape(B, T // tm, tm, a.shape[1]).transpose(0, 1, 3, 2)
    for li in range(depth):
        j = li // 2
        g = norm_g[li][None, :]
        last = li == depth - 1
        gf = final_norm_g[None, :] if last else None
        if li % 2 == 0:
            lam_init = 0.8 - 0.6 * math.exp(-0.3 * li)
            (qa, kva, ka, va, sg, qi, kif, kib, wi, qb, kvb, kb, vb) = _even_proj(
                x2, g, *w_even[j], cos, sin, tm)
            oa = _dsa_prompt(tr(qa), tr(qi), tr(wi), r3(kib), r3(ka), tile_tr(va), r3(sg[:, :W_A]), tm)
            ob = _diff_prompt(tr(qb), r3(kb), tile_tr(vb), r3(sg[:, W_A:]), lam_even[j],
                              subln_g_even[j][None, :], lam_init, tm)
            res = _out_proj([oa.reshape(M, W_A), ob.reshape(M, W_B)], w_out_even[j], x2, gf, tm)
            a_kv.append(kva.reshape(B, T, 2, H_A, HEAD_DIM))
            a_idx.append(kif.reshape(B, T, D_IDX))
            b_kv.append(kvb.reshape(B, T, 2, H_B, 2 * DH_B))
        else:
            assert T % MOBA_BLOCK == 0
            q, kv, k, v, sg, ksum = _odd_proj(x2, g, w_odd[j], cos, sin, MOBA_BLOCK)
            nb = T // MOBA_BLOCK
            kmean = ksum.reshape(B, nb, W_C) * (1.0 / MOBA_BLOCK)
            kmean = jnp.pad(kmean, ((0, 0), (0, LANES - nb), (0, 0)))
            o = _moba_prompt(tr(q), kmean, r3(k), tile_tr(v), r3(sg))
            res = _out_proj([o.reshape(M, W_C)], w_out_odd[j], x2, gf, tm)
            c_kv.append(kv.reshape(B, T, 2, H_C, HEAD_DIM))
        x2 = res[0]
        if last:
            y = res[1]
    return (y.reshape(B, T, D), jnp.stack(a_kv, axis=1), jnp.stack(a_idx, axis=1),
            jnp.stack(b_kv, axis=1), jnp.stack(c_kv, axis=1))


def kernel(x_prompt, x_sample, cache_a_kv, cache_a_idx, cache_b_kv, cache_c_kv, page_table, norm_g,
           w_in_even, w_out_even, lam_even, subln_g_even, w_in_odd, w_out_odd, final_norm_g):
    w_even = [_split_even_weight(w_in_even[j]) for j in range(w_in_even.shape[0])]
    w_odd = [w_in_odd[j].astype(BF16) for j in range(w_in_odd.shape[0])]
    w_out_e = [w_out_even[j].astype(BF16) for j in range(w_out_even.shape[0])]
    w_out_o = [w_out_odd[j].astype(BF16) for j in range(w_out_odd.shape[0])]
    weights = (norm_g, w_even, w_out_e, lam_even, subln_g_even, w_odd, w_out_o, final_norm_g)
    y_p, a_kv_p, a_idx_p, b_kv_p, c_kv_p = _prompt_trunk(x_prompt, *weights)
    past_len = page_table.shape[1] * cache_a_idx.shape[2]
    y_s, a_kv_s, a_idx_s, b_kv_s, c_kv_s = _sample_trunk(
        x_sample, past_len, (cache_a_kv, cache_a_idx, cache_b_kv, cache_c_kv), page_table, *weights)
    return (y_p, y_s, a_kv_p, a_idx_p, b_kv_p, c_kv_p, a_kv_s, a_idx_s, b_kv_s, c_kv_s)
```

```python
import functools
import math

import jax
import jax.numpy as jnp
from jax import lax
from jax.experimental import pallas as pl
from jax.experimental.pallas import tpu as pltpu

F32 = jnp.float32
BF16 = jnp.bfloat16
I32 = jnp.int32

HEAD_DIM = 64
H_A = 8
H_IDX = 8
D_IDX = 64
TOPK_MAX = 256
H_B = 4
DH_B = 64
H_C = 16
MOBA_BLOCK = 256
MOBA_TOPB = 3
ROPE_THETA = 10000.0
NORM_EPS = 1e-6
SUBLN_EPS = 1e-5
W_A = H_A * HEAD_DIM
W_B = H_B * 2 * DH_B
W_C = H_C * HEAD_DIM
IDX_COLS = H_IDX * D_IDX + D_IDX + H_IDX
IDX_PAD = 640
LANES = 128
NEG = -1e30
INT_MIN = -2 ** 31
VMEM_LIMIT = 56 * 1024 * 1024


def _cparams(*sem):
    return pltpu.CompilerParams(dimension_semantics=sem, vmem_limit_bytes=VMEM_LIMIT)


def _resident(block_shape, index_map):
    return pl.BlockSpec(block_shape, index_map, pipeline_mode=pl.Buffered(1))


def _rms(x, g, eps):
    ms = jnp.mean(x * x, axis=-1, keepdims=True)
    return x * lax.rsqrt(ms + eps) * g


def _silu(g):
    return g / (1.0 + jnp.exp(-g))


def _rope128(xc, cos, sin, first_half):
    sw = jnp.where(first_half, pltpu.roll(xc, 96, 1), pltpu.roll(xc, 32, 1))
    return xc * cos + sw * sin


def _rope_wide(y, cos, sin, first_half):
    return jnp.concatenate(
        [_rope128(y[:, c * LANES:(c + 1) * LANES], cos, sin, first_half) for c in range(y.shape[1] // LANES)],
        axis=1)


def _first_half_mask(rows):
    lane = lax.broadcasted_iota(I32, (rows, LANES), 1)
    return (lane % HEAD_DIM) < (HEAD_DIM // 2)


def _flash_tile(q, k, v, bias, m, l, acc):
    s = lax.dot_general(q, k, (((1,), (1,)), ((), ())), preferred_element_type=F32)
    if bias is not None:
        s = s + bias
    m_new = jnp.maximum(m, jnp.max(s, axis=1, keepdims=True))
    alpha = jnp.exp(m - m_new)
    p = jnp.exp(s - m_new)
    l = alpha * l + jnp.sum(p, axis=1, keepdims=True)
    acc = alpha * acc + jnp.dot(p.astype(BF16), v, preferred_element_type=F32)
    return m_new, l, acc


def _causal_bias(rows, cols):
    r = lax.broadcasted_iota(I32, (rows, cols), 0)
    c = lax.broadcasted_iota(I32, (rows, cols), 1)
    return jnp.where(c <= r, 0.0, NEG).astype(F32)


def _sortable_key(score):
    bits = lax.bitcast_convert_type(score, I32)
    return jnp.where(bits < 0, bits ^ jnp.int32(0x7FFFFFFF), bits)


def _even_proj_kernel(x_ref, g_ref, wa_ref, wi_ref, wb_ref, cos_ref, sin_ref,
                      qa_ref, kva_ref, ka_ref, va_ref, sg_ref, qi_ref, kif_ref, kib_ref, wio_ref,
                      qb_ref, kvb_ref, kb_ref, vb_ref):
    h = _rms(x_ref[...], g_ref[...], NORM_EPS).astype(BF16)
    cos = cos_ref[...]
    sin = sin_ref[...]
    fh = _first_half_mask(h.shape[0])
    scale = HEAD_DIM ** -0.5

    def proj(w_ref, lo, hi):
        return jnp.dot(h, w_ref[:, lo:hi], preferred_element_type=F32)

    for w_ref, q_ref, kv_ref, k_ref, v_ref, g_lo in ((wa_ref, qa_ref, kva_ref, ka_ref, va_ref, 0),
                                                      (wb_ref, qb_ref, kvb_ref, kb_ref, vb_ref, W_A)):
        q = _rope_wide(proj(w_ref, 0, 512), cos, sin, fh)
        q_ref[...] = (q * scale).astype(BF16)
        k = _rope_wide(proj(w_ref, 512, 1024), cos, sin, fh)
        v = proj(w_ref, 1024, 1536)
        kv_ref[:, 0:512] = k
        kv_ref[:, 512:1024] = v
        k_ref[...] = k.astype(BF16)
        v_ref[...] = v.astype(BF16)
        sg_ref[:, g_lo:g_lo + 512] = _silu(proj(w_ref, 1536, 2048))

    qi_ref[...] = _rope_wide(proj(wi_ref, 0, 512), cos, sin, fh).astype(BF16)
    tail = proj(wi_ref, 512, IDX_PAD)
    ki = _rope128(tail, cos, sin, fh)[:, 0:D_IDX]
    kif_ref[...] = ki
    kib_ref[...] = ki.astype(BF16)
    wio_ref[...] = tail[:, D_IDX:D_IDX + H_IDX]


def _even_proj(x2, g, wa, wi, wb, cos, sin, tm):
    M, D = x2.shape
    row = lambda i: (i, 0)
    const = lambda i: (0, 0)
    out_cols = ((512, BF16), (1024, F32), (512, BF16), (512, BF16), (1024, F32), (512, BF16), (D_IDX, F32),
                (D_IDX, BF16), (H_IDX, F32), (512, BF16), (1024, F32), (512, BF16), (512, BF16))
    return pl.pallas_call(
        _even_proj_kernel,
        grid=(M // tm,),
        in_specs=[pl.BlockSpec((tm, D), row), pl.BlockSpec((1, D), const),
                  _resident(wa.shape, const), _resident(wi.shape, const), _resident(wb.shape, const),
                  pl.BlockSpec((tm, LANES), row), pl.BlockSpec((tm, LANES), row)],
        out_specs=[pl.BlockSpec((tm, c), row) for c, _ in out_cols],
        out_shape=[jax.ShapeDtypeStruct((M, c), dt) for c, dt in out_cols],
        compiler_params=_cparams("parallel"),
        name="even_proj",
    )(x2, g, wa, wi, wb, cos, sin)


def _odd_proj_kernel(x_ref, g_ref, w_ref, cos_ref, sin_ref, q_ref, kv_ref, k_ref, v_ref, sg_ref, ksum_ref):
    h = _rms(x_ref[...], g_ref[...], NORM_EPS).astype(BF16)
    cos = cos_ref[...]
    sin = sin_ref[...]
    fh = _first_half_mask(h.shape[0])
    scale = HEAD_DIM ** -0.5
    for c in range(W_C // 512):
        lo = c * 512
        q = _rope_wide(jnp.dot(h, w_ref[:, lo:lo + 512], preferred_element_type=F32), cos, sin, fh)
        q_ref[:, lo:lo + 512] = (q * scale).astype(BF16)
        k = _rope_wide(jnp.dot(h, w_ref[:, W_C + lo:W_C + lo + 512], preferred_element_type=F32), cos, sin, fh)
        kv_ref[:, lo:lo + 512] = k
        k_ref[:, lo:lo + 512] = k.astype(BF16)
        ksum_ref[:, lo:lo + 512] = jnp.sum(k, axis=0, keepdims=True)
        v = jnp.dot(h, w_ref[:, 2 * W_C + lo:2 * W_C + lo + 512], preferred_element_type=F32)
        kv_ref[:, W_C + lo:W_C + lo + 512] = v
        v_ref[:, lo:lo + 512] = v.astype(BF16)
        sg_ref[:, lo:lo + 512] = _silu(jnp.dot(h, w_ref[:, 3 * W_C + lo:3 * W_C + lo + 512],
                                               preferred_element_type=F32))


def _odd_proj(x2, g, w, cos, sin, tm):
    M, D = x2.shape
    row = lambda i: (i, 0)
    const = lambda i: (0, 0)
    out_cols = ((W_C, BF16), (2 * W_C, F32), (W_C, BF16), (W_C, BF16), (W_C, F32))
    return pl.pallas_call(
        _odd_proj_kernel,
        grid=(M // tm,),
        in_specs=[pl.BlockSpec((tm, D), row), pl.BlockSpec((1, D), const), _resident(w.shape, const),
                  pl.BlockSpec((tm, LANES), row), pl.BlockSpec((tm, LANES), row)],
        out_specs=[pl.BlockSpec((tm, c), row) for c, _ in out_cols]
                  + [pl.BlockSpec((None, 1, W_C), lambda i: (i, 0, 0))],
        out_shape=[jax.ShapeDtypeStruct((M, c), dt) for c, dt in out_cols]
                  + [jax.ShapeDtypeStruct((M // tm, 1, W_C), F32)],
        compiler_params=_cparams("parallel"),
        name="odd_proj",
    )(x2, g, w, cos, sin)


def _out_proj_kernel(*refs, n_in, final):
    m_refs = refs[:n_in]
    w_ref, x_ref = refs[n_in], refs[n_in + 1]
    acc = x_ref[...]
    lo = 0
    for m_ref in m_refs:
        kk = m_ref.shape[1]
        acc = acc + jnp.dot(m_ref[...], w_ref[lo:lo + kk, :], preferred_element_type=F32)
        lo += kk
    if final:
        gf_ref, o_ref, y_ref = refs[n_in + 2:]
        y_ref[...] = _rms(acc, gf_ref[...], NORM_EPS)
    else:
        o_ref = refs[n_in + 2]
    o_ref[...] = acc


def _out_proj(mixed, w, x2, gf, tm):
    M, D = x2.shape
    row = lambda i: (i, 0)
    const = lambda i: (0, 0)
    final = gf is not None
    in_specs = [pl.BlockSpec((tm, m.shape[1]), row) for m in mixed]
    in_specs += [_resident(w.shape, const), pl.BlockSpec((tm, D), row)]
    args = list(mixed) + [w, x2]
    out_specs = [pl.BlockSpec((tm, D), row)]
    out_shape = [jax.ShapeDtypeStruct((M, D), F32)]
    if final:
        in_specs.append(pl.BlockSpec((1, D), const))
        args.append(gf)
        out_specs.append(pl.BlockSpec((tm, D), row))
        out_shape.append(jax.ShapeDtypeStruct((M, D), F32))
    return pl.pallas_call(
        functools.partial(_out_proj_kernel, n_in=len(mixed), final=final),
        grid=(M // tm,), in_specs=in_specs, out_specs=out_specs, out_shape=out_shape,
        compiler_params=_cparams("parallel"),
        name="out_proj",
    )(*args)


def _topk_select(key_ref, thr_ref, tie_ref, nch, rows, topk, col_bits, rb):
    n_acc = 4
    for r0 in range(0, rows, rb):
        def count(pred):
            accs = [jnp.zeros((rb, LANES), I32) for _ in range(n_acc)]
            for c in range(nch):
                accs[c % n_acc] = accs[c % n_acc] + pred(key_ref[c, r0:r0 + rb, :], c).astype(I32)
            return jnp.sum(sum(accs[1:], accs[0]), axis=1, keepdims=True)

        def count_ge(cand):
            cb = jnp.broadcast_to(cand, (rb, LANES))
            return count(lambda blk, c: blk >= cb)

        t0 = jnp.full((rb, 1), INT_MIN, I32)
        t0 = jnp.where(count_ge(jnp.zeros((rb, 1), I32)) >= topk, 0, t0)

        def bit_step(i, t):
            cand = t | jnp.left_shift(jnp.int32(1), 30 - i)
            return jnp.where(count_ge(cand) >= topk, cand, t)

        t = lax.fori_loop(0, 31, bit_step, t0)
        tb = jnp.broadcast_to(t, (rb, LANES))
        n_gt = count(lambda blk, c: blk > tb)
        n_ge = count(lambda blk, c: blk >= tb)
        need = topk - n_gt
        thr_ref[r0:r0 + rb, :] = tb
        tie_ref[r0:r0 + rb, :] = jnp.full((rb, LANES), 2 ** 30, I32)

        @pl.when(jnp.max(n_ge) > topk)
        def _():
            lane = lax.broadcasted_iota(I32, (rb, LANES), 1)

            def tie_step(i, vmax):
                cand = vmax | jnp.left_shift(jnp.int32(1), col_bits - 1 - i)
                cb = jnp.broadcast_to(cand, (rb, LANES))
                below = count(lambda blk, c: (blk == tb) & (c * LANES + lane < cb))
                return jnp.where(below < need, cand, vmax)

            tie = lax.fori_loop(0, col_bits, tie_step, jnp.zeros((rb, 1), I32))
            tie = jnp.where(n_ge > topk, tie, 2 ** 30)
            tie_ref[r0:r0 + rb, :] = jnp.broadcast_to(tie, (rb, LANES))


def _select_bias(key, thr, tie, col):
    sel = (key > thr) | ((key == thr) & (col <= tie) & (key != INT_MIN))
    return jnp.where(sel, 0.0, NEG).astype(F32)


def _block_diag_qt(qt_pair, tq):
    z = jnp.zeros((HEAD_DIM, tq), qt_pair.dtype)
    return jnp.concatenate([jnp.concatenate([qt_pair[0:HEAD_DIM], z], axis=1),
                            jnp.concatenate([z, qt_pair[HEAD_DIM:2 * HEAD_DIM]], axis=1)], axis=0)


def _flash_units_t(ss, vts, m_ref, l_ref, acc_ref):
    tk, tq = ss[0].shape[0], ss[0].shape[1] // 2
    ps, alphas = [], []
    for u, s in enumerate(ss):
        m = m_ref[u]
        m_new = jnp.maximum(m, jnp.max(s, axis=0, keepdims=True))
        alphas.append(jnp.exp(m - m_new))
        ps.append(jnp.exp((s - m_new).astype(BF16)))
        m_ref[u] = m_new
    ones = jnp.ones((8, tk), BF16)
    for u, pb in enumerate(ps):
        if len(vts[u]) == 1:
            pv = jnp.dot(vts[u][0], pb, preferred_element_type=F32)
        else:
            pv = jnp.concatenate([jnp.dot(vts[u][0], pb[:, :tq], preferred_element_type=F32),
                                  jnp.dot(vts[u][1], pb[:, tq:], preferred_element_type=F32)], axis=1)
        l_ref[u] = alphas[u] * l_ref[u] + jnp.dot(ones, pb, preferred_element_type=F32)[0:1]
        acc_ref[u] = alphas[u] * acc_ref[u] + pv


def _flash_init_t(m_ref, l_ref, acc_ref):
    m_ref[...] = jnp.full(m_ref.shape, NEG, F32)
    l_ref[...] = jnp.zeros(l_ref.shape, F32)
    acc_ref[...] = jnp.zeros(acc_ref.shape, F32)


def _flash_scratch_t(n_units, dv, tq):
    return [pltpu.VMEM((n_units, LANES, 2 * tq), BF16), pltpu.VMEM((n_units, 1, 2 * tq), F32),
            pltpu.VMEM((n_units, 1, 2 * tq), F32), pltpu.VMEM((n_units, dv, 2 * tq), F32)]


def _causal_bias_t(tq):
    key = lax.broadcasted_iota(I32, (tq, tq), 0)
    qry = lax.broadcasted_iota(I32, (tq, tq), 1)
    b = jnp.where(key <= qry, 0.0, NEG).astype(F32)
    return jnp.concatenate([b, b], axis=1)


def _pair_to_rows(o, tq):
    return jnp.concatenate([o[:, :tq], o[:, tq:]], axis=0).T


def _topk_select_t(key_ref, n_rows, tq, topk, col_bits):
    rb = tq
    n_acc = 4
    nblk = n_rows // rb
    rowi = lax.broadcasted_iota(I32, (8, tq), 0)

    def count(pred):
        def body(c, accs):
            r0 = pl.multiple_of(c * rb, rb)
            accs = list(accs)
            blk = key_ref[pl.ds(r0, rb), :]
            for g in range(rb // 8):
                hit = pred(blk[8 * g:8 * g + 8], r0 + 8 * g)
                accs[g % n_acc] = accs[g % n_acc] + hit.astype(I32)
            return tuple(accs)
        accs = lax.fori_loop(0, nblk, body, tuple(jnp.zeros((8, tq), I32) for _ in range(n_acc)))
        return jnp.sum(sum(accs[1:], accs[0]), axis=0, keepdims=True)

    def count_ge(cand):
        cb = jnp.broadcast_to(cand, (8, tq))
        return count(lambda blk, r0: blk >= cb)

    t0 = jnp.where(count_ge(jnp.zeros((1, tq), I32)) >= topk, 0, INT_MIN).astype(I32)

    def bit_step(b, t):
        cand = t | jnp.left_shift(jnp.int32(1), 30 - b)
        return jnp.where(count_ge(cand) >= topk, cand, t)

    thr = lax.fori_loop(0, 31, bit_step, t0)
    tb = jnp.broadcast_to(thr, (8, tq))
    n_gt = count(lambda blk, r0: blk > tb)
    n_ge = count(lambda blk, r0: blk >= tb)
    need = topk - n_gt

    def tie_search():
        def tie_step(b, vmax):
            cand = vmax | jnp.left_shift(jnp.int32(1), col_bits - 1 - b)
            cb = jnp.broadcast_to(cand, (8, tq))
            below = count(lambda blk, r0: (blk == tb) & (r0 + rowi < cb))
            return jnp.where(below < need, cand, vmax)
        tie = lax.fori_loop(0, col_bits, tie_step, jnp.zeros((1, tq), I32))
        return jnp.where(n_ge > topk, tie, 2 ** 30)

    tie = lax.cond(jnp.max(n_ge) > topk, tie_search, lambda: jnp.full((1, tq), 2 ** 30, I32))
    return thr, tie


def _dsa_prompt_kernel(qat_ref, qit_ref, wit_ref, ki_ref, ka_ref, vat_ref, sg_ref, o_ref, key_ref,
                       qbd_ref, m_ref, l_ref, acc_ref, *, tq, topk, col_bits):
    i = pl.program_id(1)
    nkt = i + 1
    n_units = W_A // LANES
    keyi = lax.broadcasted_iota(I32, (tq, tq), 0)
    qryi = lax.broadcasted_iota(I32, (tq, tq), 1)
    wit = wit_ref[...]

    def score_tile(kt, carry):
        start = pl.multiple_of(kt * tq, tq)
        ki = ki_ref[pl.ds(start, tq), :]
        score = jnp.zeros((tq, tq), F32)
        for h in range(H_IDX):
            rel = jnp.dot(ki, qit_ref[h * D_IDX:(h + 1) * D_IDX, :], preferred_element_type=F32)
            score = score + wit[h:h + 1, :] * jnp.maximum(rel, 0.0)
        key = _sortable_key(score)
        key_ref[pl.ds(start, tq), :] = jnp.where((kt < i) | (keyi <= qryi), key, INT_MIN)
        return carry

    lax.fori_loop(0, nkt, score_tile, 0)
    thr, tie = _topk_select_t(key_ref, nkt * tq, tq, topk, col_bits)
    for u in range(n_units):
        qbd_ref[u] = _block_diag_qt(qat_ref[u * LANES:(u + 1) * LANES, :], tq)
    _flash_init_t(m_ref, l_ref, acc_ref)

    def kv_tile(kt, carry):
        start = pl.multiple_of(kt * tq, tq)
        key = key_ref[pl.ds(start, tq), :]
        sel = (key > thr) | ((key == thr) & (kt * tq + keyi <= tie) & (key != INT_MIN))
        bias = jnp.where(sel, 0.0, NEG).astype(F32)
        bias = jnp.concatenate([bias, bias], axis=1)
        ss, vts = [], []
        for u in range(n_units):
            ss.append(jnp.dot(ka_ref[pl.ds(start, tq), u * LANES:(u + 1) * LANES], qbd_ref[u],
                              preferred_element_type=F32) + bias)
            vts.append([vat_ref[kt, u * LANES:u * LANES + HEAD_DIM, :],
                        vat_ref[kt, u * LANES + HEAD_DIM:(u + 1) * LANES, :]])
        _flash_units_t(ss, vts, m_ref, l_ref, acc_ref)
        return carry

    lax.fori_loop(0, nkt, kv_tile, 0)
    for u in range(n_units):
        o = _pair_to_rows(acc_ref[u] / l_ref[u], tq) * sg_ref[:, u * LANES:(u + 1) * LANES]
        o_ref[:, u * LANES:(u + 1) * LANES] = o.astype(BF16)


def _dsa_prompt(qat, qit, wit, ki, ka, vat, sg, tq):
    B, _, T = qat.shape
    L = ka.shape[1]
    topk = min(TOPK_MAX, L // 4)
    assert T == L and T % tq == 0 and tq % LANES == 0
    qtmap = lambda b, i: (b, 0, i)
    kmap = lambda b, i: (b, 0, 0)
    return pl.pallas_call(
        functools.partial(_dsa_prompt_kernel, tq=tq, topk=topk, col_bits=max(1, (L - 1).bit_length())),
        grid=(B, T // tq),
        in_specs=[pl.BlockSpec((None, W_A, tq), qtmap), pl.BlockSpec((None, H_IDX * D_IDX, tq), qtmap),
                  pl.BlockSpec((None, H_IDX, tq), qtmap),
                  _resident((None, L, D_IDX), kmap), _resident((None, L, W_A), kmap),
                  _resident((None, L // tq, W_A, tq), lambda b, i: (b, 0, 0, 0)),
                  pl.BlockSpec((None, tq, W_A), lambda b, i: (b, i, 0))],
        out_specs=pl.BlockSpec((None, tq, W_A), lambda b, i: (b, i, 0)),
        out_shape=jax.ShapeDtypeStruct((B, T, W_A), BF16),
        scratch_shapes=[pltpu.VMEM((L, tq), I32)] + _flash_scratch_t(W_A // LANES, HEAD_DIM, tq),
        compiler_params=_cparams("parallel", "arbitrary"),
        name="dsa_prompt",
    )(qat, qit, wit, ki, ka, vat, sg)


def _lambda_value(lam_ref, lam_init):
    lam = lam_ref[...]
    s1 = jnp.sum(lam[0:1] * lam[1:2], axis=1, keepdims=True)
    s2 = jnp.sum(lam[2:3] * lam[3:4], axis=1, keepdims=True)
    return jnp.exp(s1) - jnp.exp(s2) + lam_init


def _diff_finish(acc1, l1, acc2, l2, lam_val, subg, sg, lam_init):
    o = acc1 / l1 - lam_val * (acc2 / l2)
    return _rms(o, subg, SUBLN_EPS) * (1.0 - lam_init) * sg


def _diff_prompt_kernel(qt_ref, k_ref, vt_ref, sg_ref, lam_ref, subg_ref, o_ref, qbd_ref, m_ref, l_ref, acc_ref,
                        *, tq, lam_init):
    i = pl.program_id(1)
    lam_val = _lambda_value(lam_ref, lam_init)
    diag = _causal_bias_t(tq)
    dv = 2 * DH_B
    for h in range(H_B):
        qbd_ref[h] = _block_diag_qt(qt_ref[h * dv:(h + 1) * dv, :], tq)
    _flash_init_t(m_ref, l_ref, acc_ref)

    def tile(kt, carry, bias):
        start = pl.multiple_of(kt * tq, tq)
        ss, vts = [], []
        for h in range(H_B):
            s = jnp.dot(k_ref[pl.ds(start, tq), h * dv:(h + 1) * dv], qbd_ref[h], preferred_element_type=F32)
            ss.append(s if bias is None else s + bias)
            vts.append([vt_ref[kt, h * dv:(h + 1) * dv, :]])
        _flash_units_t(ss, vts, m_ref, l_ref, acc_ref)
        return carry

    lax.fori_loop(0, i, lambda kt, c: tile(kt, c, None), 0)
    tile(i, 0, diag)
    for h in range(H_B):
        o = acc_ref[h] / l_ref[h]
        o = (o[:, :tq] - lam_val * o[:, tq:]).T
        o = _rms(o, subg_ref[...], SUBLN_EPS) * (1.0 - lam_init) * sg_ref[:, h * dv:(h + 1) * dv]
        o_ref[:, h * dv:(h + 1) * dv] = o.astype(BF16)


def _diff_prompt(qbt, kb, vbt, sg, lam, subg, lam_init, tq):
    B, _, T = qbt.shape
    L = kb.shape[1]
    assert T == L and T % tq == 0
    qmap = lambda b, i: (b, i, 0)
    kmap = lambda b, i: (b, 0, 0)
    const = lambda b, i: (0, 0)
    return pl.pallas_call(
        functools.partial(_diff_prompt_kernel, tq=tq, lam_init=lam_init),
        grid=(B, T // tq),
        in_specs=[pl.BlockSpec((None, W_B, tq), lambda b, i: (b, 0, i)), _resident((None, L, W_B), kmap),
                  _resident((None, L // tq, W_B, tq), lambda b, i: (b, 0, 0, 0)),
                  pl.BlockSpec((None, tq, W_B), qmap),
                  pl.BlockSpec(lam.shape, const), pl.BlockSpec(subg.shape, const)],
        out_specs=pl.BlockSpec((None, tq, W_B), qmap),
        out_shape=jax.ShapeDtypeStruct((B, T, W_B), BF16),
        scratch_shapes=_flash_scratch_t(H_B, 2 * DH_B, tq),
        compiler_params=_cparams("parallel", "arbitrary"),
        name="diff_prompt",
    )(qbt, kb, vbt, sg, lam, subg)


def _top_blocks(gate, n_valid, n_sel):
    lane = lax.broadcasted_iota(I32, gate.shape, 1)
    g = jnp.where(lane < n_valid, gate, -jnp.inf)
    sel = jnp.zeros(gate.shape, F32)
    for _ in range(n_sel):
        mx = jnp.max(g, axis=1, keepdims=True)
        first = jnp.min(jnp.where(g == mx, lane, LANES), axis=1, keepdims=True)
        pick = (lane == first) & (lane < n_valid)
        sel = jnp.where(pick, 1.0, sel)
        g = jnp.where(lane == first, -jnp.inf, g)
    return sel


def _top_blocks_t(gate, n_valid, n_sel):
    blk = lax.broadcasted_iota(I32, gate.shape, 0)
    g = jnp.where(blk < n_valid, gate, -jnp.inf)
    sel = jnp.zeros(gate.shape, F32)
    for _ in range(n_sel):
        mx = jnp.max(g, axis=0, keepdims=True)
        first = jnp.min(jnp.where(g == mx, blk, LANES), axis=0, keepdims=True)
        pick = (blk == first) & (blk < n_valid)
        sel = jnp.where(pick, 1.0, sel)
        g = jnp.where(blk == first, -jnp.inf, g)
    return sel


def _moba_prompt_kernel(qt_ref, kmean_ref, k_ref, vt_ref, sg_ref, o_ref, bias_ref, qbd_ref, m_ref, l_ref, acc_ref,
                        *, tq, n_sel, n_units):
    i = pl.program_id(2)
    diag = _causal_bias_t(tq)
    for u in range(n_units):
        qbd = _block_diag_qt(qt_ref[u * LANES:(u + 1) * LANES, :], tq)
        gate = jnp.dot(kmean_ref[:, u * LANES:(u + 1) * LANES].astype(BF16), qbd,
                       preferred_element_type=F32)
        bias_ref[u] = (_top_blocks_t(gate, i, n_sel) - 1.0) * (-NEG)
        qbd_ref[u] = qbd
    _flash_init_t(m_ref, l_ref, acc_ref)

    def tile(kt, carry, diagonal):
        start = pl.multiple_of(kt * tq, tq)
        ss, vts = [], []
        for u in range(n_units):
            bias = diag if diagonal else bias_ref[u, pl.ds(kt, 1), :]
            ss.append(jnp.dot(k_ref[pl.ds(start, tq), u * LANES:(u + 1) * LANES], qbd_ref[u],
                              preferred_element_type=F32) + bias)
            vts.append([vt_ref[kt, u * LANES:u * LANES + HEAD_DIM, :],
                        vt_ref[kt, u * LANES + HEAD_DIM:(u + 1) * LANES, :]])
        _flash_units_t(ss, vts, m_ref, l_ref, acc_ref)
        return carry

    lax.fori_loop(0, i, lambda kt, c: tile(kt, c, False), 0)
    tile(i, 0, True)
    for u in range(n_units):
        o = _pair_to_rows(acc_ref[u] / l_ref[u], tq) * sg_ref[:, u * LANES:(u + 1) * LANES]
        o_ref[:, u * LANES:(u + 1) * LANES] = o.astype(BF16)


def _moba_prompt(qt, kmean, k, vt, sg, hg=8):
    B, _, T = qt.shape
    L = k.shape[1]
    tq = MOBA_BLOCK
    nb = L // tq
    assert T == L and L % tq == 0 and nb <= LANES
    n_sel = max(1, min(MOBA_TOPB, nb - 1))
    wg = hg * HEAD_DIM
    n_units = wg // LANES
    qmap = lambda b, g, i: (b, i, g)
    kmap = lambda b, g, i: (b, 0, g)
    return pl.pallas_call(
        functools.partial(_moba_prompt_kernel, tq=tq, n_sel=n_sel, n_units=n_units),
        grid=(B, W_C // wg, T // tq),
        in_specs=[pl.BlockSpec((None, wg, tq), lambda b, g, i: (b, g, i)),
                  _resident((None, LANES, wg), kmap), _resident((None, L, wg), kmap),
                  _resident((None, nb, wg, tq), lambda b, g, i: (b, 0, g, 0)),
                  pl.BlockSpec((None, tq, wg), qmap)],
        out_specs=pl.BlockSpec((None, tq, wg), qmap),
        out_shape=jax.ShapeDtypeStruct((B, T, W_C), BF16),
        scratch_shapes=[pltpu.VMEM((n_units, LANES, 2 * tq), F32)] + _flash_scratch_t(n_units, HEAD_DIM, tq),
        compiler_params=_cparams("parallel", "parallel", "arbitrary"),
        name="moba_prompt",
    )(qt, kmean, k, vt, sg)


def _page_map(layer, n_pages):
    def index_map(b, p, pt):
        return (pt[b, jnp.minimum(p, n_pages - 1)], layer, 0, 0)
    return index_map


def _pad_rows(a, rows):
    return jnp.concatenate([a, jnp.zeros((rows - a.shape[0], a.shape[1]), a.dtype)], axis=0)


def _new_token_bias(nq):
    r = lax.broadcasted_iota(I32, (nq, LANES), 0)
    c = lax.broadcasted_iota(I32, (nq, LANES), 1)
    return jnp.where(c <= r, 0.0, NEG).astype(F32)


def _fold_heads(o, n_heads, nq, width):
    col_head = lax.broadcasted_iota(I32, (nq, n_heads * width), 1) // width
    out = jnp.zeros((nq, n_heads * width), F32)
    for h in range(n_heads):
        out = jnp.where(col_head == h, o[h * nq:(h + 1) * nq, :], out)
    return out


def _flash_ref_update(q, k, v, bias, m_ref, l_ref, acc_ref):
    m, l, acc = _flash_tile(q, k, v, bias, m_ref[...], l_ref[...], acc_ref[...])
    m_ref[...] = m
    l_ref[...] = l
    acc_ref[...] = acc


def _flash_ref_init(m_ref, l_ref, acc_ref):
    m_ref[...] = jnp.full(m_ref.shape, NEG, F32)
    l_ref[...] = jnp.zeros(l_ref.shape, F32)
    acc_ref[...] = jnp.zeros(acc_ref.shape, F32)


def _dsa_sample_select_kernel(pt_ref, qi_ref, wi_ref, kc_ref, kn_ref, bias_ref, key_ref, thr_ref, tie_ref,
                              *, n_pages, nq, topk, col_bits):
    p = pl.program_id(1)
    qi = qi_ref[...]
    wi = wi_ref[...]

    def score(kblk):
        rel = lax.dot_general(qi, kblk, (((1,), (1,)), ((), ())), preferred_element_type=F32)
        wr = wi * jnp.maximum(rel, 0.0)
        s = jnp.zeros((nq, kblk.shape[0]), F32)
        for h in range(H_IDX):
            s = s + wr[h * nq:(h + 1) * nq, :]
        return s

    @pl.when(p < n_pages)
    def _():
        key_ref[p] = _sortable_key(score(kc_ref[...].astype(BF16)))

    @pl.when(p == n_pages)
    def _():
        s = score(_pad_rows(kn_ref[...], LANES).astype(BF16))
        r = lax.broadcasted_iota(I32, (nq, LANES), 0)
        lane = lax.broadcasted_iota(I32, (nq, LANES), 1)
        key_ref[n_pages] = jnp.where(lane <= r, _sortable_key(s), INT_MIN)
        _topk_select(key_ref, thr_ref, tie_ref, n_pages + 1, nq, topk, col_bits, rb=nq)
        thr = thr_ref[...]
        tie = tie_ref[...]

        def bias_chunk(c, carry):
            bias_ref[c] = _select_bias(key_ref[c], thr, tie, c * LANES + lane)
            return carry

        lax.fori_loop(0, n_pages + 1, bias_chunk, 0)


def _dsa_sample_select(page_table, layer, qi_s, wi_s, cache_idx, ki_new):
    Bd, n_pages = page_table.shape
    nq = ki_new.shape[1]
    L = n_pages * cache_idx.shape[2] + nq
    topk = min(TOPK_MAX, L // 4)
    assert cache_idx.shape[2] == LANES and nq == 8
    bmap = lambda b, p, pt: (b, 0, 0)
    return pl.pallas_call(
        functools.partial(_dsa_sample_select_kernel, n_pages=n_pages, nq=nq, topk=topk,
                          col_bits=max(1, (L - 1).bit_length())),
        grid_spec=pltpu.PrefetchScalarGridSpec(
            num_scalar_prefetch=1, grid=(Bd, n_pages + 1),
            in_specs=[pl.BlockSpec((None,) + qi_s.shape[1:], bmap), pl.BlockSpec((None,) + wi_s.shape[1:], bmap),
                      pl.BlockSpec((None, None, LANES, D_IDX), _page_map(layer, n_pages)),
                      pl.BlockSpec((None, nq, D_IDX), bmap)],
            out_specs=pl.BlockSpec((None, n_pages + 1, nq, LANES), lambda b, p, pt: (b, 0, 0, 0)),
            scratch_shapes=[pltpu.VMEM((n_pages + 1, nq, LANES), I32), pltpu.VMEM((nq, LANES), I32),
                            pltpu.VMEM((nq, LANES), I32)]),
        out_shape=jax.ShapeDtypeStruct((Bd, n_pages + 1, nq, LANES), F32),
        compiler_params=_cparams("parallel", "arbitrary"),
        name="dsa_sample_select",
    )(page_table, qi_s, wi_s, cache_idx, ki_new)


def _dsa_sample_attn_kernel(pt_ref, q_ref, kvc_ref, kvn_ref, bias_ref, sg_ref, o_ref, m_ref, l_ref, acc_ref,
                            *, n_pages, nq):
    p = pl.program_id(1)
    q = q_ref[...]
    bias = jnp.concatenate([bias_ref[...]] * H_A, axis=0)

    @pl.when(p == 0)
    def _():
        _flash_ref_init(m_ref, l_ref, acc_ref)

    @pl.when(p < n_pages)
    def _():
        _flash_ref_update(q, kvc_ref[:, 0:W_A].astype(BF16), kvc_ref[:, W_A:2 * W_A].astype(BF16), bias,
                          m_ref, l_ref, acc_ref)

    @pl.when(p == n_pages)
    def _():
        kv = _pad_rows(kvn_ref[...], LANES).astype(BF16)
        _flash_ref_update(q, kv[:, 0:W_A], kv[:, W_A:2 * W_A], bias, m_ref, l_ref, acc_ref)
        o = _fold_heads(acc_ref[...] / l_ref[...], H_A, nq, HEAD_DIM)
        o_ref[...] = (o * sg_ref[...]).astype(BF16)


def _dsa_sample_attn(page_table, layer, q_bd, cache_kv, kv_new, bias, sg):
    Bd, n_pages = page_table.shape
    nq = kv_new.shape[1]
    rows = q_bd.shape[1]
    bmap = lambda b, p, pt: (b, 0, 0)
    return pl.pallas_call(
        functools.partial(_dsa_sample_attn_kernel, n_pages=n_pages, nq=nq),
        grid_spec=pltpu.PrefetchScalarGridSpec(
            num_scalar_prefetch=1, grid=(Bd, n_pages + 1),
            in_specs=[pl.BlockSpec((None, rows, W_A), bmap),
                      pl.BlockSpec((None, None, LANES, 2 * W_A), _page_map(layer, n_pages)),
                      pl.BlockSpec((None, nq, 2 * W_A), bmap),
                      pl.BlockSpec((None, None, nq, LANES), lambda b, p, pt: (b, p, 0, 0)),
                      pl.BlockSpec((None, nq, W_A), bmap)],
            out_specs=pl.BlockSpec((None, nq, W_A), bmap),
            scratch_shapes=[pltpu.VMEM((rows, 1), F32), pltpu.VMEM((rows, 1), F32), pltpu.VMEM((rows, W_A), F32)]),
        out_shape=jax.ShapeDtypeStruct((Bd, nq, W_A), BF16),
        compiler_params=_cparams("parallel", "arbitrary"),
        name="dsa_sample_attn",
    )(page_table, q_bd, cache_kv, kv_new, bias, sg)


def _diff_sample_kernel(pt_ref, q_ref, kvc_ref, kvn_ref, sg_ref, lam_ref, subg_ref, o_ref, m_ref, l_ref, acc_ref,
                        *, n_pages, nq, lam_init):
    p = pl.program_id(1)
    q = q_ref[...]

    @pl.when(p == 0)
    def _():
        _flash_ref_init(m_ref, l_ref, acc_ref)

    @pl.when(p < n_pages)
    def _():
        _flash_ref_update(q, kvc_ref[:, 0:W_B].astype(BF16), kvc_ref[:, W_B:2 * W_B].astype(BF16), None,
                          m_ref, l_ref, acc_ref)

    @pl.when(p == n_pages)
    def _():
        kv = _pad_rows(kvn_ref[...], LANES).astype(BF16)
        bias = jnp.concatenate([_new_token_bias(nq)] * (2 * H_B), axis=0)
        _flash_ref_update(q, kv[:, 0:W_B], kv[:, W_B:2 * W_B], bias, m_ref, l_ref, acc_ref)
        lam_val = _lambda_value(lam_ref, lam_init)
        dv = 2 * DH_B
        l = l_ref[...]
        for h in range(H_B):
            r1, r2 = (2 * h) * nq, (2 * h + 1) * nq
            o = _diff_finish(acc_ref[r1:r1 + nq, h * dv:(h + 1) * dv], l[r1:r1 + nq],
                             acc_ref[r2:r2 + nq, h * dv:(h + 1) * dv], l[r2:r2 + nq],
                             lam_val, subg_ref[...], sg_ref[:, h * dv:(h + 1) * dv], lam_init)
            o_ref[:, h * dv:(h + 1) * dv] = o.astype(BF16)


def _diff_sample(page_table, layer, q_bd, cache_kv, kv_new, sg, lam, subg, lam_init):
    Bd, n_pages = page_table.shape
    nq = kv_new.shape[1]
    rows = q_bd.shape[1]
    bmap = lambda b, p, pt: (b, 0, 0)
    const = lambda b, p, pt: (0, 0)
    return pl.pallas_call(
        functools.partial(_diff_sample_kernel, n_pages=n_pages, nq=nq, lam_init=lam_init),
        grid_spec=pltpu.PrefetchScalarGridSpec(
            num_scalar_prefetch=1, grid=(Bd, n_pages + 1),
            in_specs=[pl.BlockSpec((None, rows, W_B), bmap),
                      pl.BlockSpec((None, None, LANES, 2 * W_B), _page_map(layer, n_pages)),
                      pl.BlockSpec((None, nq, 2 * W_B), bmap),
                      pl.BlockSpec((None, nq, W_B), bmap),
                      pl.BlockSpec(lam.shape, const), pl.BlockSpec(subg.shape, const)],
            out_specs=pl.BlockSpec((None, nq, W_B), bmap),
            scratch_shapes=[pltpu.VMEM((rows, 1), F32), pltpu.VMEM((rows, 1), F32), pltpu.VMEM((rows, W_B), F32)]),
        out_shape=jax.ShapeDtypeStruct((Bd, nq, W_B), BF16),
        compiler_params=_cparams("parallel", "arbitrary"),
        name="diff_sample",
    )(page_table, q_bd, cache_kv, kv_new, sg, lam, subg)


def _moba_sample_gate_kernel(pt_ref, q_ref, kc_ref, sel_ref, ksum_ref, gate_ref, *, n_pages, ppb, n_sel):
    p = pl.program_id(1)
    colsum = jnp.sum(kc_ref[...], axis=0, keepdims=True)

    @pl.when(p % ppb == 0)
    def _():
        ksum_ref[...] = colsum

    @pl.when(p % ppb != 0)
    def _():
        ksum_ref[...] = ksum_ref[...] + colsum

    @pl.when(p == 0)
    def _():
        gate_ref[...] = jnp.zeros(gate_ref.shape, F32)

    @pl.when(p % ppb == ppb - 1)
    def _():
        mean = (ksum_ref[...] * (1.0 / MOBA_BLOCK)).astype(BF16)
        g = lax.dot_general(q_ref[...], jnp.broadcast_to(mean, (8, mean.shape[1])), (((1,), (1,)), ((), ())),
                            preferred_element_type=F32)
        lane = lax.broadcasted_iota(I32, gate_ref.shape, 1)
        gate_ref[...] = jnp.where(lane == p // ppb, g[:, 0:1], gate_ref[...])

    @pl.when(p == n_pages - 1)
    def _():
        sel_ref[...] = _top_blocks(gate_ref[...], n_pages // ppb, n_sel)


def _moba_sample_gate(page_table, layer, q_bd, cache_kv):
    Bd, n_pages = page_table.shape
    rows = q_bd.shape[1]
    ppb = MOBA_BLOCK // LANES
    nbp = n_pages // ppb
    assert n_pages % ppb == 0 and nbp <= LANES
    n_sel = max(1, min(MOBA_TOPB, nbp))
    bmap = lambda b, p, pt: (b, 0, 0)
    return pl.pallas_call(
        functools.partial(_moba_sample_gate_kernel, n_pages=n_pages, ppb=ppb, n_sel=n_sel),
        grid_spec=pltpu.PrefetchScalarGridSpec(
            num_scalar_prefetch=1, grid=(Bd, n_pages),
            in_specs=[pl.BlockSpec((None, rows, W_C), bmap),
                      pl.BlockSpec((None, None, LANES, W_C), _page_map(layer, n_pages))],
            out_specs=pl.BlockSpec((None, rows, LANES), bmap),
            scratch_shapes=[pltpu.VMEM((1, W_C), F32), pltpu.VMEM((rows, LANES), F32)]),
        out_shape=jax.ShapeDtypeStruct((Bd, rows, LANES), F32),
        compiler_params=_cparams("parallel", "arbitrary"),
        name="moba_sample_gate",
    )(page_table, q_bd, cache_kv)


def _moba_sample_attn_kernel(pt_ref, q_ref, kvc_ref, kvn_ref, sel_ref, sg_ref, o_ref, m_ref, l_ref, acc_ref,
                             *, n_pages, nq, ppb):
    p = pl.program_id(1)
    q = q_ref[...]

    @pl.when(p == 0)
    def _():
        _flash_ref_init(m_ref, l_ref, acc_ref)

    @pl.when(p < n_pages)
    def _():
        blk_row = lax.broadcasted_iota(I32, (LANES, LANES), 0)
        onehot = jnp.where(blk_row == p // ppb, 1.0, 0.0).astype(BF16)
        picked = jnp.dot(sel_ref[...].astype(BF16), onehot, preferred_element_type=F32)
        bias = (picked - 1.0) * (-NEG)
        _flash_ref_update(q, kvc_ref[:, 0:W_C].astype(BF16), kvc_ref[:, W_C:2 * W_C].astype(BF16), bias,
                          m_ref, l_ref, acc_ref)

    @pl.when(p == n_pages)
    def _():
        kv = _pad_rows(kvn_ref[...], LANES).astype(BF16)
        bias = jnp.concatenate([_new_token_bias(nq)] * H_C, axis=0)
        _flash_ref_update(q, kv[:, 0:W_C], kv[:, W_C:2 * W_C], bias, m_ref, l_ref, acc_ref)
        o = _fold_heads(acc_ref[...] / l_ref[...], H_C, nq, HEAD_DIM)
        o_ref[...] = (o * sg_ref[...]).astype(BF16)


def _moba_sample_attn(page_table, layer, q_bd, cache_kv, kv_new, sel, sg):
    Bd, n_pages = page_table.shape
    nq = kv_new.shape[1]
    rows = q_bd.shape[1]
    bmap = lambda b, p, pt: (b, 0, 0)
    return pl.pallas_call(
        functools.partial(_moba_sample_attn_kernel, n_pages=n_pages, nq=nq, ppb=MOBA_BLOCK // LANES),
        grid_spec=pltpu.PrefetchScalarGridSpec(
            num_scalar_prefetch=1, grid=(Bd, n_pages + 1),
            in_specs=[pl.BlockSpec((None, rows, W_C), bmap),
                      pl.BlockSpec((None, None, LANES, 2 * W_C), _page_map(layer, n_pages)),
                      pl.BlockSpec((None, nq, 2 * W_C), bmap),
                      pl.BlockSpec((None, rows, LANES), bmap),
                      pl.BlockSpec((None, nq, W_C), bmap)],
            out_specs=pl.BlockSpec((None, nq, W_C), bmap),
            scratch_shapes=[pltpu.VMEM((rows, 1), F32), pltpu.VMEM((rows, 1), F32), pltpu.VMEM((rows, W_C), F32)]),
        out_shape=jax.ShapeDtypeStruct((Bd, nq, W_C), BF16),
        compiler_params=_cparams("parallel", "arbitrary"),
        name="moba_sample_attn",
    )(page_table, q_bd, cache_kv, kv_new, sel, sg)


def _pages_per_step(n_pages, cap):
    pps = cap
    while n_pages % pps:
        pps //= 2
    return pps


def _page_specs(block, layer, pps):
    def spec(j):
        def index_map(b, s, pt):
            return (pt[b, s * pps + j], layer) + (0,) * (len(block) - 2)
        return pl.BlockSpec(block, index_map)
    return [spec(j) for j in range(pps)]


def _head_match_bias(n_heads, nq, n_slots):
    r = lax.broadcasted_iota(I32, (n_heads * nq, n_slots * n_heads), 0)
    c = lax.broadcasted_iota(I32, (n_heads * nq, n_slots * n_heads), 1)
    return jnp.where(c % n_heads == r // nq, 0.0, NEG).astype(F32)


def _new_token_flat_bias(n_heads, nq):
    r = lax.broadcasted_iota(I32, (n_heads * nq, nq * n_heads), 0)
    c = lax.broadcasted_iota(I32, (n_heads * nq, nq * n_heads), 1)
    ok = (c % n_heads == r // nq) & (c // n_heads <= r % nq)
    return jnp.where(ok, 0.0, NEG).astype(F32)


def _flash_rows(q, k, v, bias, idx, m_ref, l_ref, acc_ref):
    m, l, acc = _flash_tile(q, k, v, bias, m_ref[idx], l_ref[idx], acc_ref[idx])
    m_ref[idx] = m
    l_ref[idx] = l
    acc_ref[idx] = acc


def _rows_to_heads(o, n_heads, nq):
    return jnp.concatenate([o[h * nq:(h + 1) * nq, :] for h in range(n_heads)], axis=1)


def _paged_step(q, ks, vs, bias, m_ref, l_ref, acc_ref, transposed):
    nt = (((1,), (1,)), ((), ()))
    cat = lambda xs, axis: jnp.concatenate(xs, axis=axis) if len(xs) > 1 else xs[0]
    if transposed:
        s = jnp.dot(q, cat(ks, 1), preferred_element_type=F32)
    else:
        s = lax.dot_general(q, cat(ks, 0), nt, preferred_element_type=F32)
    if bias is not None:
        s = s + bias
    m_prev = m_ref[...]
    m_new = jnp.maximum(m_prev, jnp.max(s, axis=1, keepdims=True))
    alpha = jnp.exp(m_prev - m_new)
    p = jnp.exp(s - m_new)
    l_ref[...] = alpha * l_ref[...] + jnp.sum(p, axis=1, keepdims=True)
    pb = p.astype(BF16)
    if transposed:
        pv = lax.dot_general(pb, cat(vs, 1), nt, preferred_element_type=F32)
    else:
        pv = jnp.dot(pb, cat(vs, 0), preferred_element_type=F32)
    acc_ref[...] = alpha * acc_ref[...] + pv
    m_ref[...] = m_new


def _dsa_select_paged_kernel(pt_ref, qi_ref, wi_ref, *refs, n_pages, pps, nq, topk, col_bits):
    kc_refs = refs[:pps]
    kn_ref, bias_ref, key_ref, thr_ref, tie_ref = refs[pps:]
    s = pl.program_id(1)
    qi = qi_ref[...]
    wi = wi_ref[...]

    def score(rel):
        wr = wi * jnp.maximum(rel, 0.0)
        out = jnp.zeros((nq, rel.shape[1]), F32)
        for h in range(H_IDX):
            out = out + wr[h * nq:(h + 1) * nq, :]
        return out

    for j in range(pps):
        rel = jnp.dot(qi, kc_refs[j][...].astype(BF16), preferred_element_type=F32)
        key_ref[s * pps + j] = _sortable_key(score(rel))

    @pl.when(s == n_pages // pps - 1)
    def _():
        kn = _pad_rows(kn_ref[...], LANES).astype(BF16)
        sc = score(lax.dot_general(qi, kn, (((1,), (1,)), ((), ())), preferred_element_type=F32))
        r = lax.broadcasted_iota(I32, (nq, LANES), 0)
        lane = lax.broadcasted_iota(I32, (nq, LANES), 1)
        key_ref[n_pages] = jnp.where(lane <= r, _sortable_key(sc), INT_MIN)
        _topk_select(key_ref, thr_ref, tie_ref, n_pages + 1, nq, topk, col_bits, rb=nq)
        thr = thr_ref[...]
        tie = tie_ref[...]

        def bias_chunk(c, carry):
            bias_ref[c] = _select_bias(key_ref[c], thr, tie, c * LANES + lane)
            return carry

        lax.fori_loop(0, n_pages + 1, bias_chunk, 0)


def _dsa_select_paged(page_table, layer, qi_s, wi_s, cache_idx_t, ki_new):
    Bd, n_pages = page_table.shape
    nq = ki_new.shape[1]
    L = n_pages * LANES + nq
    topk = min(TOPK_MAX, L // 4)
    pps = _pages_per_step(n_pages, 8)
    assert cache_idx_t.shape[3] == LANES and nq == 8
    bmap = lambda b, s, pt: (b, 0, 0)
    return pl.pallas_call(
        functools.partial(_dsa_select_paged_kernel, n_pages=n_pages, pps=pps, nq=nq, topk=topk,
                          col_bits=max(1, (L - 1).bit_length())),
        grid_spec=pltpu.PrefetchScalarGridSpec(
            num_scalar_prefetch=1, grid=(Bd, n_pages // pps),
            in_specs=[pl.BlockSpec((None,) + qi_s.shape[1:], bmap), pl.BlockSpec((None,) + wi_s.shape[1:], bmap)]
                     + _page_specs((None, None, D_IDX, LANES), layer, pps)
                     + [pl.BlockSpec((None, nq, D_IDX), bmap)],
            out_specs=pl.BlockSpec((None, n_pages + 1, nq, LANES), lambda b, s, pt: (b, 0, 0, 0)),
            scratch_shapes=[pltpu.VMEM((n_pages + 1, nq, LANES), I32), pltpu.VMEM((nq, LANES), I32),
                            pltpu.VMEM((nq, LANES), I32)]),
        out_shape=jax.ShapeDtypeStruct((Bd, n_pages + 1, nq, LANES), F32),
        compiler_params=_cparams("parallel", "arbitrary"),
        name="dsa_sample_select",
    )(page_table, qi_s, wi_s, *([cache_idx_t] * pps), ki_new)


def _dsa_attn_paged_kernel(pt_ref, q_ref, bias_ref, biasn_ref, *refs, n_pages, pps, nq):
    kv_refs = refs[:pps]
    kvn_ref, sg_ref, o_ref, m_ref, l_ref, acc_ref = refs[pps:]
    s = pl.program_id(1)
    q = q_ref[...]

    @pl.when(s == 0)
    def _():
        _flash_ref_init(m_ref, l_ref, acc_ref)

    ks = [kv_refs[j][0].reshape(W_A, LANES).astype(BF16) for j in range(pps)]
    vs = [kv_refs[j][1].reshape(W_A, LANES).astype(BF16) for j in range(pps)]
    bias = jnp.concatenate([jnp.concatenate([bias_ref[j]] * H_A, axis=0) for j in range(pps)], axis=1)
    _paged_step(q, ks, vs, bias, m_ref, l_ref, acc_ref, transposed=True)

    @pl.when(s == n_pages // pps - 1)
    def _():
        kv = _pad_rows(kvn_ref[...], LANES).astype(BF16)
        bias_n = jnp.concatenate([biasn_ref[...]] * H_A, axis=0)
        _paged_step(q, [kv[:, 0:W_A]], [kv[:, W_A:2 * W_A]], bias_n, m_ref, l_ref, acc_ref, transposed=False)
        o = _fold_heads(acc_ref[...] / l_ref[...], H_A, nq, HEAD_DIM)
        o_ref[...] = (o * sg_ref[...]).astype(BF16)


def _dsa_attn_paged(page_table, layer, q_bd, cache_kv_t, kv_new, bias, sg):
    Bd, n_pages = page_table.shape
    nq = kv_new.shape[1]
    rows = q_bd.shape[1]
    pps = _pages_per_step(n_pages, 8)
    bmap = lambda b, s, pt: (b, 0, 0)
    return pl.pallas_call(
        functools.partial(_dsa_attn_paged_kernel, n_pages=n_pages, pps=pps, nq=nq),
        grid_spec=pltpu.PrefetchScalarGridSpec(
            num_scalar_prefetch=1, grid=(Bd, n_pages // pps),
            in_specs=[pl.BlockSpec((None, rows, W_A), bmap),
                      pl.BlockSpec((None, pps, nq, LANES), lambda b, s, pt: (b, s, 0, 0)),
                      pl.BlockSpec((None, None, nq, LANES), lambda b, s, pt: (b, n_pages, 0, 0))]
                     + _page_specs((None, None, 2, H_A, HEAD_DIM, LANES), layer, pps)
                     + [pl.BlockSpec((None, nq, 2 * W_A), bmap), pl.BlockSpec((None, nq, W_A), bmap)],
            out_specs=pl.BlockSpec((None, nq, W_A), bmap),
            scratch_shapes=[pltpu.VMEM((rows, 1), F32), pltpu.VMEM((rows, 1), F32), pltpu.VMEM((rows, W_A), F32)]),
        out_shape=jax.ShapeDtypeStruct((Bd, nq, W_A), BF16),
        compiler_params=_cparams("parallel", "arbitrary"),
        name="dsa_sample_attn",
    )(page_table, q_bd, bias, bias, *([cache_kv_t] * pps), kv_new, sg)


def _diff_paged_kernel(pt_ref, q_ref, *refs, n_pages, pps, nq, lam_init):
    kv_refs = refs[:pps]
    kvn_ref, sg_ref, lam_ref, subg_ref, o_ref, m_ref, l_ref, acc_ref = refs[pps:]
    s = pl.program_id(1)
    dv = 2 * DH_B
    rows_per_slot = 2 * H_B

    @pl.when(s == 0)
    def _():
        _flash_ref_init(m_ref, l_ref, acc_ref)

    for h in range(H_B):
        ks = [kv_refs[j][pl.ds(h, LANES, stride=rows_per_slot), :].astype(BF16) for j in range(pps)]
        vs = [kv_refs[j][pl.ds(H_B + h, LANES, stride=rows_per_slot), :].astype(BF16) for j in range(pps)]
        _paged_step(q_ref[h], ks, vs, None, m_ref.at[h], l_ref.at[h], acc_ref.at[h], transposed=False)

    @pl.when(s == n_pages // pps - 1)
    def _():
        kv = _pad_rows(kvn_ref[...], LANES).astype(BF16)
        bias = jnp.concatenate([_new_token_bias(nq)] * 2, axis=0)
        lam_val = _lambda_value(lam_ref, lam_init)
        for h in range(H_B):
            _paged_step(q_ref[h], [kv[:, h * dv:(h + 1) * dv]], [kv[:, W_B + h * dv:W_B + (h + 1) * dv]], bias,
                        m_ref.at[h], l_ref.at[h], acc_ref.at[h], transposed=False)
            l = l_ref[h]
            o = _diff_finish(acc_ref[h, 0:nq, :], l[0:nq], acc_ref[h, nq:2 * nq, :], l[nq:2 * nq],
                             lam_val, subg_ref[...], sg_ref[:, h * dv:(h + 1) * dv], lam_init)
            o_ref[:, h * dv:(h + 1) * dv] = o.astype(BF16)


def _diff_paged(page_table, layer, q_h, cache_rows, kv_new, sg, lam, subg, lam_init):
    Bd, n_pages = page_table.shape
    nq = kv_new.shape[1]
    dv = 2 * DH_B
    pps = _pages_per_step(n_pages, 8)
    bmap = lambda b, s, pt: (b, 0, 0)
    const = lambda b, s, pt: (0, 0)
    return pl.pallas_call(
        functools.partial(_diff_paged_kernel, n_pages=n_pages, pps=pps, nq=nq, lam_init=lam_init),
        grid_spec=pltpu.PrefetchScalarGridSpec(
            num_scalar_prefetch=1, grid=(Bd, n_pages // pps),
            in_specs=[pl.BlockSpec((None, H_B, 2 * nq, dv), lambda b, s, pt: (b, 0, 0, 0))]
                     + _page_specs((None, None, LANES * 2 * H_B, dv), layer, pps)
                     + [pl.BlockSpec((None, nq, 2 * W_B), bmap), pl.BlockSpec((None, nq, W_B), bmap),
                        pl.BlockSpec(lam.shape, const), pl.BlockSpec(subg.shape, const)],
            out_specs=pl.BlockSpec((None, nq, W_B), bmap),
            scratch_shapes=[pltpu.VMEM((H_B, 2 * nq, 1), F32), pltpu.VMEM((H_B, 2 * nq, 1), F32),
                            pltpu.VMEM((H_B, 2 * nq, dv), F32)]),
        out_shape=jax.ShapeDtypeStruct((Bd, nq, W_B), BF16),
        compiler_params=_cparams("parallel", "arbitrary"),
        name="diff_sample",
    )(page_table, q_h, *([cache_rows] * pps), kv_new, sg, lam, subg)


def _moba_gate_paged_kernel(pt_ref, q_ref, *refs, n_pages, pps, ppb, n_sel):
    kc_refs = refs[:pps]
    sel_ref, gate_ref = refs[pps:]
    s = pl.program_id(1)
    q = q_ref[...]

    @pl.when(s == 0)
    def _():
        gate_ref[...] = jnp.zeros(gate_ref.shape, F32)

    lane = lax.broadcasted_iota(I32, gate_ref.shape, 1)
    for bi in range(pps // ppb):
        tot = None
        for j in range(bi * ppb, (bi + 1) * ppb):
            sc = jnp.dot(q, kc_refs[j][...].reshape(W_C, LANES).astype(BF16), preferred_element_type=F32)
            rs = jnp.sum(sc, axis=1, keepdims=True)
            tot = rs if tot is None else tot + rs
        gate_ref[...] = jnp.where(lane == s * (pps // ppb) + bi, tot * (1.0 / MOBA_BLOCK), gate_ref[...])

    @pl.when(s == n_pages // pps - 1)
    def _():
        sel_ref[...] = _top_blocks(gate_ref[...], n_pages // ppb, n_sel)


def _moba_gate_paged(page_table, layer, q_bd, cache_kv_t):
    Bd, n_pages = page_table.shape
    rows = q_bd.shape[1]
    ppb = MOBA_BLOCK // LANES
    nbp = n_pages // ppb
    pps = _pages_per_step(n_pages, 8)
    assert n_pages % ppb == 0 and pps % ppb == 0 and nbp <= LANES
    n_sel = max(1, min(MOBA_TOPB, nbp))
    bmap = lambda b, s, pt: (b, 0, 0)

    def kspec(j):
        def index_map(b, s, pt):
            return (pt[b, s * pps + j], layer, 0, 0, 0, 0)
        return pl.BlockSpec((None, None, None, H_C, HEAD_DIM, LANES), index_map)

    return pl.pallas_call(
        functools.partial(_moba_gate_paged_kernel, n_pages=n_pages, pps=pps, ppb=ppb, n_sel=n_sel),
        grid_spec=pltpu.PrefetchScalarGridSpec(
            num_scalar_prefetch=1, grid=(Bd, n_pages // pps),
            in_specs=[pl.BlockSpec((None, rows, W_C), bmap)] + [kspec(j) for j in range(pps)],
            out_specs=pl.BlockSpec((None, rows, LANES), bmap),
            scratch_shapes=[pltpu.VMEM((rows, LANES), F32)]),
        out_shape=jax.ShapeDtypeStruct((Bd, rows, LANES), F32),
        compiler_params=_cparams("parallel", "arbitrary"),
        name="moba_sample_gate",
    )(page_table, q_bd, *([cache_kv_t] * pps))


def _moba_attn_paged_kernel(pt_ref, q_ref, sel_ref, *refs, n_pages, pps, ppb, nq):
    kv_refs = refs[:pps]
    kvn_ref, sg_ref, o_ref, m_ref, l_ref, acc_ref = refs[pps:]
    s = pl.program_id(1)
    q = q_ref[...]
    blk_row = lax.broadcasted_iota(I32, (LANES, LANES), 0)

    @pl.when(s == 0)
    def _():
        _flash_ref_init(m_ref, l_ref, acc_ref)

    sel = sel_ref[...].astype(BF16)
    bias = []
    for j in range(pps):
        onehot = jnp.where(blk_row == (s * pps + j) // ppb, 1.0, 0.0).astype(BF16)
        picked = jnp.dot(sel, onehot, preferred_element_type=F32)
        bias.append((picked - 1.0) * (-NEG))
    ks = [kv_refs[j][0].reshape(W_C, LANES).astype(BF16) for j in range(pps)]
    vs = [kv_refs[j][1].reshape(W_C, LANES).astype(BF16) for j in range(pps)]
    _paged_step(q, ks, vs, jnp.concatenate(bias, axis=1), m_ref, l_ref, acc_ref, transposed=True)

    @pl.when(s == n_pages // pps - 1)
    def _():
        kv = _pad_rows(kvn_ref[...], LANES).astype(BF16)
        bias_n = jnp.concatenate([_new_token_bias(nq)] * H_C, axis=0)
        _paged_step(q, [kv[:, 0:W_C]], [kv[:, W_C:2 * W_C]], bias_n, m_ref, l_ref, acc_ref, transposed=False)
        o = _fold_heads(acc_ref[...] / l_ref[...], H_C, nq, HEAD_DIM)
        o_ref[...] = (o * sg_ref[...]).astype(BF16)


def _moba_attn_paged(page_table, layer, q_bd, cache_kv_t, kv_new, sel, sg):
    Bd, n_pages = page_table.shape
    nq = kv_new.shape[1]
    rows = q_bd.shape[1]
    pps = _pages_per_step(n_pages, 8)
    bmap = lambda b, s, pt: (b, 0, 0)
    return pl.pallas_call(
        functools.partial(_moba_attn_paged_kernel, n_pages=n_pages, pps=pps, ppb=MOBA_BLOCK // LANES, nq=nq),
        grid_spec=pltpu.PrefetchScalarGridSpec(
            num_scalar_prefetch=1, grid=(Bd, n_pages // pps),
            in_specs=[pl.BlockSpec((None, rows, W_C), bmap), pl.BlockSpec((None, rows, LANES), bmap)]
                     + _page_specs((None, None, 2, H_C, HEAD_DIM, LANES), layer, pps)
                     + [pl.BlockSpec((None, nq, 2 * W_C), bmap), pl.BlockSpec((None, nq, W_C), bmap)],
            out_specs=pl.BlockSpec((None, nq, W_C), bmap),
            scratch_shapes=[pltpu.VMEM((rows, 1), F32), pltpu.VMEM((rows, 1), F32), pltpu.VMEM((rows, W_C), F32)]),
        out_shape=jax.ShapeDtypeStruct((Bd, nq, W_C), BF16),
        compiler_params=_cparams("parallel", "arbitrary"),
        name="moba_sample_attn",
    )(page_table, q_bd, sel, *([cache_kv_t] * pps), kv_new, sg)


def _block_diag_queries(q2, Bd, nq, n_heads, width):
    q4 = q2.reshape(Bd, nq, n_heads, width)
    eye = jnp.eye(n_heads, dtype=q2.dtype)
    return jnp.einsum('bqhd,hg->bhqgd', q4, eye).reshape(Bd, n_heads * nq, n_heads * width)


def _sample_trunk(x, past_len, caches, page_table, norm_g, w_even, w_out_even, lam_even, subln_g_even,
                  w_odd, w_out_odd, final_norm_g):
    cache_a_kv, cache_a_idx, cache_b_kv, cache_c_kv = caches
    Bd, nq, D = x.shape
    M = Bd * nq
    n_pool, _, page, _ = cache_a_idx.shape
    assert page == LANES and past_len % MOBA_BLOCK == 0 and nq <= 8
    cb_rows = cache_b_kv.reshape(n_pool, cache_b_kv.shape[1], page * 2 * H_B, 2 * DH_B)
    ca_kv_t = cache_a_kv.transpose(0, 1, 3, 4, 5, 2)
    ca_idx_t = cache_a_idx.transpose(0, 1, 3, 2)
    cc_kv_t = cache_c_kv.transpose(0, 1, 3, 4, 5, 2)
    depth = norm_g.shape[0]
    heads_first = lambda a, nh: a.reshape(Bd, nq, nh, HEAD_DIM).transpose(0, 2, 1, 3).reshape(Bd, nh * nq, HEAD_DIM)
    cos, sin = _rope_tables(past_len + jnp.arange(nq, dtype=I32))
    cos, sin = jnp.tile(cos, (Bd, 1)), jnp.tile(sin, (Bd, 1))
    x2 = x.reshape(M, D)
    a_kv, a_idx, b_kv, c_kv = [], [], [], []
    y = None
    r3 = lambda a: a.reshape(Bd, nq, a.shape[1])
    for li in range(depth):
        j = li // 2
        g = norm_g[li][None, :]
        last = li == depth - 1
        gf = final_norm_g[None, :] if last else None
        if li % 2 == 0:
            lam_init = 0.8 - 0.6 * math.exp(-0.3 * li)
            (qa, kva, _, _, sg, qi, kif, _, wi, qb, kvb, _, _) = _even_proj(x2, g, *w_even[j], cos, sin, M)
            wi_s = wi.reshape(Bd, nq, H_IDX).transpose(0, 2, 1).reshape(Bd, H_IDX * nq, 1)
            bias = _dsa_select_paged(page_table, j, heads_first(qi, H_IDX), wi_s, ca_idx_t, r3(kif))
            oa = _dsa_attn_paged(page_table, j, _block_diag_queries(qa, Bd, nq, H_A, HEAD_DIM), ca_kv_t,
                                 r3(kva), bias, r3(sg[:, :W_A]))
            qb_h = jnp.einsum('bqhmd,mn->bhmqnd', qb.reshape(Bd, nq, H_B, 2, DH_B), jnp.eye(2, dtype=qb.dtype))
            ob = _diff_paged(page_table, j, qb_h.reshape(Bd, H_B, 2 * nq, 2 * DH_B), cb_rows, r3(kvb),
                             r3(sg[:, W_A:]), lam_even[j], subln_g_even[j][None, :], lam_init)
            res = _out_proj([oa.reshape(M, W_A), ob.reshape(M, W_B)], w_out_even[j], x2, gf, M)
            a_kv.append(kva.reshape(Bd, nq, 2, H_A, HEAD_DIM))
            a_idx.append(kif.reshape(Bd, nq, D_IDX))
            b_kv.append(kvb.reshape(Bd, nq, 2, H_B, 2 * DH_B))
        else:
            q, kv, _, _, sg, _ = _odd_proj(x2, g, w_odd[j], cos, sin, M)
            q_bd = _block_diag_queries(q, Bd, nq, H_C, HEAD_DIM)
            sel = _moba_gate_paged(page_table, j, q_bd, cc_kv_t)
            o = _moba_attn_paged(page_table, j, q_bd, cc_kv_t, r3(kv), sel, r3(sg))
            res = _out_proj([o.reshape(M, W_C)], w_out_odd[j], x2, gf, M)
            c_kv.append(kv.reshape(Bd, nq, 2, H_C, HEAD_DIM))
        x2 = res[0]
        if last:
            y = res[1]
    return (y.reshape(Bd, nq, D), jnp.stack(a_kv, axis=1), jnp.stack(a_idx, axis=1),
            jnp.stack(b_kv, axis=1), jnp.stack(c_kv, axis=1))


def _rope_tables(pos):
    half = HEAD_DIM // 2
    inv_freq = jnp.exp(-math.log(ROPE_THETA) * jnp.arange(half, dtype=F32) / half)
    ang = pos.astype(F32)[:, None] * inv_freq[None, :]
    cos, sin = jnp.cos(ang), jnp.sin(ang)
    return jnp.concatenate([cos, cos, cos, cos], axis=1), jnp.concatenate([-sin, sin, -sin, sin], axis=1)


def _split_even_weight(w):
    wa = w[:, 0:4 * W_A].astype(BF16)
    wi = jnp.pad(w[:, 4 * W_A:4 * W_A + IDX_COLS], ((0, 0), (0, IDX_PAD - IDX_COLS))).astype(BF16)
    wb = w[:, 4 * W_A + IDX_COLS:].astype(BF16)
    return wa, wi, wb


def _prompt_trunk(x, norm_g, w_even, w_out_even, lam_even, subln_g_even, w_odd, w_out_odd, final_norm_g):
    B, T, D = x.shape
    M = B * T
    tm = min(256, T)
    depth = norm_g.shape[0]
    cos, sin = _rope_tables(jnp.arange(T, dtype=I32))
    cos, sin = jnp.tile(cos, (B, 1)), jnp.tile(sin, (B, 1))
    x2 = x.reshape(M, D)
    a_kv, a_idx, b_kv, c_kv = [], [], [], []
    y = None
    r3 = lambda a: a.reshape(B, T, a.shape[1])
    tr = lambda a: r3(a).transpose(0, 2, 1)
    tile_tr = lambda a: a.reshape(B, T // tm, tm, a.shape[1]).transpose(0, 1, 3, 2)
    for li in range(depth):
        j = li // 2
        g = norm_g[li][None, :]
        last = li == depth - 1
        gf = final_norm_g[None, :] if last else None
        if li % 2 == 0:
            lam_init = 0.8 - 0.6 * math.exp(-0.3 * li)
            (qa, kva, ka, va, sg, qi, kif, kib, wi, qb, kvb, kb, vb) = _even_proj(
                x2, g, *w_even[j], cos, sin, tm)
            oa = _dsa_prompt(tr(qa), tr(qi), tr(wi), r3(kib), r3(ka), tile_tr(va), r3(sg[:, :W_A]), tm)
            ob = _diff_prompt(tr(qb), r3(kb), tile_tr(vb), r3(sg[:, W_A:]), lam_even[j],
                              subln_g_even[j][None, :], lam_init, tm)
            res = _out_proj([oa.reshape(M, W_A), ob.reshape(M, W_B)], w_out_even[j], x2, gf, tm)
            a_kv.append(kva.reshape(B, T, 2, H_A, HEAD_DIM))
            a_idx.append(kif.reshape(B, T, D_IDX))
            b_kv.append(kvb.reshape(B, T, 2, H_B, 2 * DH_B))
        else:
            assert T % MOBA_BLOCK == 0
            q, kv, k, v, sg, ksum = _odd_proj(x2, g, w_odd[j], cos, sin, MOBA_BLOCK)
            nb = T // MOBA_BLOCK
            kmean = ksum.reshape(B, nb, W_C) * (1.0 / MOBA_BLOCK)
            kmean = jnp.pad(kmean, ((0, 0), (0, LANES - nb), (0, 0)))
            o = _moba_prompt(tr(q), kmean, r3(k), tile_tr(v), r3(sg))
            res = _out_proj([o.reshape(M, W_C)], w_out_odd[j], x2, gf, tm)
            c_kv.append(kv.reshape(B, T, 2, H_C, HEAD_DIM))
        x2 = res[0]
        if last:
            y = res[1]
    return (y.reshape(B, T, D), jnp.stack(a_kv, axis=1), jnp.stack(a_idx, axis=1),
            jnp.stack(b_kv, axis=1), jnp.stack(c_kv, axis=1))


def kernel(x_prompt, x_sample, cache_a_kv, cache_a_idx, cache_b_kv, cache_c_kv, page_table, norm_g,
           w_in_even, w_out_even, lam_even, subln_g_even, w_in_odd, w_out_odd, final_norm_g):
    w_even = [_split_even_weight(w_in_even[j]) for j in range(w_in_even.shape[0])]
    w_odd = [w_in_odd[j].astype(BF16) for j in range(w_in_odd.shape[0])]
    w_out_e = [w_out_even[j].astype(BF16) for j in range(w_out_even.shape[0])]
    w_out_o = [w_out_odd[j].astype(BF16) for j in range(w_out_odd.shape[0])]
    weights = (norm_g, w_even, w_out_e, lam_even, subln_g_even, w_odd, w_out_o, final_norm_g)
    y_p, a_kv_p, a_idx_p, b_kv_p, c_kv_p = _prompt_trunk(x_prompt, *weights)
    past_len = page_table.shape[1] * cache_a_idx.shape[2]
    y_s, a_kv_s, a_idx_s, b_kv_s, c_kv_s = _sample_trunk(
        x_sample, past_len, (cache_a_kv, cache_a_idx, cache_b_kv, cache_c_kv), page_table, *weights)
    return (y_p, y_s, a_kv_p, a_idx_p, b_kv_p, c_kv_p, a_kv_s, a_idx_s, b_kv_s, c_kv_s)
```

```python
import functools
import math

import jax
import jax.numpy as jnp
from jax import lax
from jax.experimental import pallas as pl
from jax.experimental.pallas import tpu as pltpu

F32 = jnp.float32
BF16 = jnp.bfloat16
I32 = jnp.int32

HEAD_DIM = 64
H_A = 8
H_IDX = 8
D_IDX = 64
TOPK_MAX = 256
H_B = 4
DH_B = 64
H_C = 16
MOBA_BLOCK = 256
MOBA_TOPB = 3
ROPE_THETA = 10000.0
NORM_EPS = 1e-6
SUBLN_EPS = 1e-5
W_A = H_A * HEAD_DIM
W_B = H_B * 2 * DH_B
W_C = H_C * HEAD_DIM
IDX_COLS = H_IDX * D_IDX + D_IDX + H_IDX
IDX_PAD = 640
LANES = 128
NEG = -1e30
INT_MIN = -2 ** 31
VMEM_LIMIT = 56 * 1024 * 1024


def _cparams(*sem):
    return pltpu.CompilerParams(dimension_semantics=sem, vmem_limit_bytes=VMEM_LIMIT)


def _resident(block_shape, index_map):
    return pl.BlockSpec(block_shape, index_map, pipeline_mode=pl.Buffered(1))


def _rms(x, g, eps):
    ms = jnp.mean(x * x, axis=-1, keepdims=True)
    return x * lax.rsqrt(ms + eps) * g


def _silu(g):
    return g / (1.0 + jnp.exp(-g))


def _rope128(xc, cos, sin, first_half):
    sw = jnp.where(first_half, pltpu.roll(xc, 96, 1), pltpu.roll(xc, 32, 1))
    return xc * cos + sw * sin


def _rope_wide(y, cos, sin, first_half):
    return jnp.concatenate(
        [_rope128(y[:, c * LANES:(c + 1) * LANES], cos, sin, first_half) for c in range(y.shape[1] // LANES)],
        axis=1)


def _first_half_mask(rows):
    lane = lax.broadcasted_iota(I32, (rows, LANES), 1)
    return (lane % HEAD_DIM) < (HEAD_DIM // 2)


def _sortable_key(score):
    bits = lax.bitcast_convert_type(score, I32)
    return jnp.where(bits < 0, bits ^ jnp.int32(0x7FFFFFFF), bits)


def _put_bf16(ref, val, lo, t_out):
    if t_out:
        ref[lo:lo + val.shape[1], :] = val.T.astype(BF16)
    else:
        ref[:, lo:lo + val.shape[1]] = val.astype(BF16)


def _transposed_out_specs(kind, width, tm, bt):
    B, T = bt
    nt = T // tm
    if kind == "q":
        return (pl.BlockSpec((None, width, tm), lambda i: (i // nt, 0, i % nt)),
                jax.ShapeDtypeStruct((B, width, T), BF16))
    return (pl.BlockSpec((None, None, width, tm), lambda i: (i // nt, i % nt, 0, 0)),
            jax.ShapeDtypeStruct((B, nt, width, tm), BF16))


def _even_proj_kernel(x_ref, g_ref, wa_ref, wi_ref, wb_ref, cos_ref, sin_ref,
                      qa_ref, kva_ref, ka_ref, va_ref, sg_ref, qi_ref, kif_ref, kib_ref, wio_ref,
                      qb_ref, kvb_ref, kb_ref, vb_ref, *, t_out):
    h = _rms(x_ref[...], g_ref[...], NORM_EPS).astype(BF16)
    cos = cos_ref[...]
    sin = sin_ref[...]
    fh = _first_half_mask(h.shape[0])
    scale = HEAD_DIM ** -0.5

    def proj(w_ref, lo, hi):
        return jnp.dot(h, w_ref[:, lo:hi], preferred_element_type=F32)

    for w_ref, q_ref, kv_ref, k_ref, v_ref, g_lo in ((wa_ref, qa_ref, kva_ref, ka_ref, va_ref, 0),
                                                      (wb_ref, qb_ref, kvb_ref, kb_ref, vb_ref, W_A)):
        q = _rope_wide(proj(w_ref, 0, 512), cos, sin, fh)
        _put_bf16(q_ref, q * scale, 0, t_out)
        k = _rope_wide(proj(w_ref, 512, 1024), cos, sin, fh)
        v = proj(w_ref, 1024, 1536)
        kv_ref[:, 0:512] = k
        kv_ref[:, 512:1024] = v
        k_ref[...] = k.astype(BF16)
        _put_bf16(v_ref, v, 0, t_out)
        sg_ref[:, g_lo:g_lo + 512] = _silu(proj(w_ref, 1536, 2048))

    _put_bf16(qi_ref, _rope_wide(proj(wi_ref, 0, 512), cos, sin, fh), 0, t_out)
    tail = proj(wi_ref, 512, IDX_PAD)
    ki = _rope128(tail, cos, sin, fh)[:, 0:D_IDX]
    kif_ref[...] = ki
    kib_ref[...] = ki.astype(BF16)
    wio_ref[...] = tail[:, D_IDX:D_IDX + H_IDX]


def _even_proj(x2, g, wa, wi, wb, cos, sin, tm, bt=None):
    M, D = x2.shape
    row = lambda i: (i, 0)
    const = lambda i: (0, 0)
    out_cols = ((512, BF16), (1024, F32), (512, BF16), (512, BF16), (1024, F32), (512, BF16), (D_IDX, F32),
                (D_IDX, BF16), (H_IDX, F32), (512, BF16), (1024, F32), (512, BF16), (512, BF16))
    out_specs = [pl.BlockSpec((tm, c), row) for c, _ in out_cols]
    out_shape = [jax.ShapeDtypeStruct((M, c), dt) for c, dt in out_cols]
    if bt is not None:
        for pos, kind in ((0, "q"), (3, "v"), (5, "q"), (9, "q"), (12, "v")):
            out_specs[pos], out_shape[pos] = _transposed_out_specs(kind, 512, tm, bt)
    return pl.pallas_call(
        functools.partial(_even_proj_kernel, t_out=bt is not None),
        grid=(M // tm,),
        in_specs=[pl.BlockSpec((tm, D), row), pl.BlockSpec((1, D), const),
                  _resident(wa.shape, const), _resident(wi.shape, const), _resident(wb.shape, const),
                  pl.BlockSpec((tm, LANES), row), pl.BlockSpec((tm, LANES), row)],
        out_specs=out_specs,
        out_shape=out_shape,
        compiler_params=_cparams("parallel"),
        name="even_proj",
    )(x2, g, wa, wi, wb, cos, sin)


def _odd_proj_kernel(x_ref, g_ref, w_ref, cos_ref, sin_ref, q_ref, kv_ref, k_ref, v_ref, sg_ref, ksum_ref,
                     *, t_out):
    h = _rms(x_ref[...], g_ref[...], NORM_EPS).astype(BF16)
    cos = cos_ref[...]
    sin = sin_ref[...]
    fh = _first_half_mask(h.shape[0])
    scale = HEAD_DIM ** -0.5
    for c in range(W_C // 512):
        lo = c * 512
        q = _rope_wide(jnp.dot(h, w_ref[:, lo:lo + 512], preferred_element_type=F32), cos, sin, fh)
        _put_bf16(q_ref, q * scale, lo, t_out)
        k = _rope_wide(jnp.dot(h, w_ref[:, W_C + lo:W_C + lo + 512], preferred_element_type=F32), cos, sin, fh)
        kv_ref[:, lo:lo + 512] = k
        k_ref[:, lo:lo + 512] = k.astype(BF16)
        ksum_ref[:, lo:lo + 512] = jnp.sum(k, axis=0, keepdims=True)
        v = jnp.dot(h, w_ref[:, 2 * W_C + lo:2 * W_C + lo + 512], preferred_element_type=F32)
        kv_ref[:, W_C + lo:W_C + lo + 512] = v
        _put_bf16(v_ref, v, lo, t_out)
        sg_ref[:, lo:lo + 512] = _silu(jnp.dot(h, w_ref[:, 3 * W_C + lo:3 * W_C + lo + 512],
                                               preferred_element_type=F32))


def _odd_proj(x2, g, w, cos, sin, tm, bt=None):
    M, D = x2.shape
    row = lambda i: (i, 0)
    const = lambda i: (0, 0)
    out_cols = ((W_C, BF16), (2 * W_C, F32), (W_C, BF16), (W_C, BF16), (W_C, F32))
    out_specs = [pl.BlockSpec((tm, c), row) for c, _ in out_cols]
    out_shape = [jax.ShapeDtypeStruct((M, c), dt) for c, dt in out_cols]
    if bt is not None:
        out_specs[0], out_shape[0] = _transposed_out_specs("q", W_C, tm, bt)
        out_specs[3], out_shape[3] = _transposed_out_specs("v", W_C, tm, bt)
    return pl.pallas_call(
        functools.partial(_odd_proj_kernel, t_out=bt is not None),
        grid=(M // tm,),
        in_specs=[pl.BlockSpec((tm, D), row), pl.BlockSpec((1, D), const), _resident(w.shape, const),
                  pl.BlockSpec((tm, LANES), row), pl.BlockSpec((tm, LANES), row)],
        out_specs=out_specs + [pl.BlockSpec((None, 1, W_C), lambda i: (i, 0, 0))],
        out_shape=out_shape + [jax.ShapeDtypeStruct((M // tm, 1, W_C), F32)],
        compiler_params=_cparams("parallel"),
        name="odd_proj",
    )(x2, g, w, cos, sin)


def _out_proj_kernel(*refs, n_in, final):
    m_refs = refs[:n_in]
    w_ref, x_ref = refs[n_in], refs[n_in + 1]
    acc = x_ref[...]
    lo = 0
    for m_ref in m_refs:
        kk = m_ref.shape[1]
        acc = acc + jnp.dot(m_ref[...], w_ref[lo:lo + kk, :], preferred_element_type=F32)
        lo += kk
    if final:
        gf_ref, o_ref, y_ref = refs[n_in + 2:]
        y_ref[...] = _rms(acc, gf_ref[...], NORM_EPS)
    else:
        o_ref = refs[n_in + 2]
    o_ref[...] = acc


def _out_proj(mixed, w, x2, gf, tm):
    M, D = x2.shape
    row = lambda i: (i, 0)
    const = lambda i: (0, 0)
    final = gf is not None
    in_specs = [pl.BlockSpec((tm, m.shape[1]), row) for m in mixed]
    in_specs += [_resident(w.shape, const), pl.BlockSpec((tm, D), row)]
    args = list(mixed) + [w, x2]
    out_specs = [pl.BlockSpec((tm, D), row)]
    out_shape = [jax.ShapeDtypeStruct((M, D), F32)]
    if final:
        in_specs.append(pl.BlockSpec((1, D), const))
        args.append(gf)
        out_specs.append(pl.BlockSpec((tm, D), row))
        out_shape.append(jax.ShapeDtypeStruct((M, D), F32))
    return pl.pallas_call(
        functools.partial(_out_proj_kernel, n_in=len(mixed), final=final),
        grid=(M // tm,), in_specs=in_specs, out_specs=out_specs, out_shape=out_shape,
        compiler_params=_cparams("parallel"),
        name="out_proj",
    )(*args)


def _topk_select(key_ref, thr_ref, tie_ref, nch, rows, topk, col_bits, rb):
    n_acc = 4
    for r0 in range(0, rows, rb):
        def count(pred):
            accs = [jnp.zeros((rb, LANES), I32) for _ in range(n_acc)]
            for c in range(nch):
                accs[c % n_acc] = accs[c % n_acc] + pred(key_ref[c, r0:r0 + rb, :], c).astype(I32)
            return jnp.sum(sum(accs[1:], accs[0]), axis=1, keepdims=True)

        def count_ge(cand):
            cb = jnp.broadcast_to(cand, (rb, LANES))
            return count(lambda blk, c: blk >= cb)

        t0 = jnp.full((rb, 1), INT_MIN, I32)
        t0 = jnp.where(count_ge(jnp.zeros((rb, 1), I32)) >= topk, 0, t0)

        def bit_step(i, t):
            cand = t | jnp.left_shift(jnp.int32(1), 30 - i)
            return jnp.where(count_ge(cand) >= topk, cand, t)

        t = lax.fori_loop(0, 31, bit_step, t0)
        tb = jnp.broadcast_to(t, (rb, LANES))
        n_gt = count(lambda blk, c: blk > tb)
        n_ge = count(lambda blk, c: blk >= tb)
        need = topk - n_gt
        thr_ref[r0:r0 + rb, :] = tb
        tie_ref[r0:r0 + rb, :] = jnp.full((rb, LANES), 2 ** 30, I32)

        @pl.when(jnp.max(n_ge) > topk)
        def _():
            lane = lax.broadcasted_iota(I32, (rb, LANES), 1)

            def tie_step(i, vmax):
                cand = vmax | jnp.left_shift(jnp.int32(1), col_bits - 1 - i)
                cb = jnp.broadcast_to(cand, (rb, LANES))
                below = count(lambda blk, c: (blk == tb) & (c * LANES + lane < cb))
                return jnp.where(below < need, cand, vmax)

            tie = lax.fori_loop(0, col_bits, tie_step, jnp.zeros((rb, 1), I32))
            tie = jnp.where(n_ge > topk, tie, 2 ** 30)
            tie_ref[r0:r0 + rb, :] = jnp.broadcast_to(tie, (rb, LANES))


def _select_bias(key, thr, tie, col):
    sel = (key > thr) | ((key == thr) & (col <= tie) & (key != INT_MIN))
    return jnp.where(sel, 0.0, NEG).astype(F32)


def _block_diag_qt(qt_pair, tq):
    z = jnp.zeros((HEAD_DIM, tq), qt_pair.dtype)
    return jnp.concatenate([jnp.concatenate([qt_pair[0:HEAD_DIM], z], axis=1),
                            jnp.concatenate([z, qt_pair[HEAD_DIM:2 * HEAD_DIM]], axis=1)], axis=0)


def _flash_units_t(ss, vts, m_ref, l_ref, acc_ref):
    tk, tq = ss[0].shape[0], ss[0].shape[1] // 2
    ps, alphas = [], []
    for u, s in enumerate(ss):
        m = m_ref[u]
        m_new = jnp.maximum(m, jnp.max(s, axis=0, keepdims=True))
        alphas.append(jnp.exp(m - m_new))
        ps.append(jnp.exp((s - m_new).astype(BF16)))
        m_ref[u] = m_new
    ones = jnp.ones((8, tk), BF16)
    for u, pb in enumerate(ps):
        if len(vts[u]) == 1:
            pv = jnp.dot(vts[u][0], pb, preferred_element_type=F32)
        else:
            pv = jnp.concatenate([jnp.dot(vts[u][0], pb[:, :tq], preferred_element_type=F32),
                                  jnp.dot(vts[u][1], pb[:, tq:], preferred_element_type=F32)], axis=1)
        l_ref[u] = alphas[u] * l_ref[u] + jnp.dot(ones, pb, preferred_element_type=F32)[0:1]
        acc_ref[u] = alphas[u] * acc_ref[u] + pv


def _flash_init_t(m_ref, l_ref, acc_ref):
    m_ref[...] = jnp.full(m_ref.shape, NEG, F32)
    l_ref[...] = jnp.zeros(l_ref.shape, F32)
    acc_ref[...] = jnp.zeros(acc_ref.shape, F32)


def _flash_scratch_t(n_units, dv, tq):
    return [pltpu.VMEM((n_units, LANES, 2 * tq), BF16), pltpu.VMEM((n_units, 1, 2 * tq), F32),
            pltpu.VMEM((n_units, 1, 2 * tq), F32), pltpu.VMEM((n_units, dv, 2 * tq), F32)]


def _causal_bias_t(tq):
    key = lax.broadcasted_iota(I32, (tq, tq), 0)
    qry = lax.broadcasted_iota(I32, (tq, tq), 1)
    b = jnp.where(key <= qry, 0.0, NEG).astype(F32)
    return jnp.concatenate([b, b], axis=1)


def _pair_to_rows(o, tq):
    return jnp.concatenate([o[:, :tq], o[:, tq:]], axis=0).T


def _topk_select_t(key_ref, n_rows, tq, topk, col_bits):
    rb = tq
    n_acc = 4
    nblk = n_rows // rb
    rowi = lax.broadcasted_iota(I32, (8, tq), 0)

    def count(pred):
        def body(c, accs):
            r0 = pl.multiple_of(c * rb, rb)
            accs = list(accs)
            blk = key_ref[pl.ds(r0, rb), :]
            for g in range(rb // 8):
                hit = pred(blk[8 * g:8 * g + 8], r0 + 8 * g)
                accs[g % n_acc] = accs[g % n_acc] + hit.astype(I32)
            return tuple(accs)
        accs = lax.fori_loop(0, nblk, body, tuple(jnp.zeros((8, tq), I32) for _ in range(n_acc)))
        return jnp.sum(sum(accs[1:], accs[0]), axis=0, keepdims=True)

    def count_ge(cand):
        cb = jnp.broadcast_to(cand, (8, tq))
        return count(lambda blk, r0: blk >= cb)

    t0 = jnp.where(count_ge(jnp.zeros((1, tq), I32)) >= topk, 0, INT_MIN).astype(I32)

    def bit_step(b, t):
        cand = t | jnp.left_shift(jnp.int32(1), 30 - b)
        return jnp.where(count_ge(cand) >= topk, cand, t)

    thr = lax.fori_loop(0, 31, bit_step, t0)
    tb = jnp.broadcast_to(thr, (8, tq))
    n_gt = count(lambda blk, r0: blk > tb)
    n_ge = count(lambda blk, r0: blk >= tb)
    need = topk - n_gt

    def tie_search():
        def tie_step(b, vmax):
            cand = vmax | jnp.left_shift(jnp.int32(1), col_bits - 1 - b)
            cb = jnp.broadcast_to(cand, (8, tq))
            below = count(lambda blk, r0: (blk == tb) & (r0 + rowi < cb))
            return jnp.where(below < need, cand, vmax)
        tie = lax.fori_loop(0, col_bits, tie_step, jnp.zeros((1, tq), I32))
        return jnp.where(n_ge > topk, tie, 2 ** 30)

    tie = lax.cond(jnp.max(n_ge) > topk, tie_search, lambda: jnp.full((1, tq), 2 ** 30, I32))
    return thr, tie


def _dsa_prompt_kernel(qat_ref, qit_ref, wit_ref, ki_ref, ka_ref, vat_ref, sg_ref, o_ref, key_ref,
                       qbd_ref, m_ref, l_ref, acc_ref, *, tq, topk, col_bits):
    i = pl.program_id(1)
    nkt = i + 1
    n_units = W_A // LANES
    keyi = lax.broadcasted_iota(I32, (tq, tq), 0)
    qryi = lax.broadcasted_iota(I32, (tq, tq), 1)
    wit = wit_ref[...]

    def score_tile(kt, carry):
        start = pl.multiple_of(kt * tq, tq)
        ki = ki_ref[pl.ds(start, tq), :]
        score = jnp.zeros((tq, tq), F32)
        for h in range(H_IDX):
            rel = jnp.dot(ki, qit_ref[h * D_IDX:(h + 1) * D_IDX, :], preferred_element_type=F32)
            score = score + wit[h:h + 1, :] * jnp.maximum(rel, 0.0)
        key = _sortable_key(score)
        key_ref[pl.ds(start, tq), :] = jnp.where((kt < i) | (keyi <= qryi), key, INT_MIN)
        return carry

    lax.fori_loop(0, nkt, score_tile, 0)
    thr, tie = _topk_select_t(key_ref, nkt * tq, tq, topk, col_bits)
    for u in range(n_units):
        qbd_ref[u] = _block_diag_qt(qat_ref[u * LANES:(u + 1) * LANES, :], tq)
    _flash_init_t(m_ref, l_ref, acc_ref)

    def kv_tile(kt, carry):
        start = pl.multiple_of(kt * tq, tq)
        key = key_ref[pl.ds(start, tq), :]
        sel = (key > thr) | ((key == thr) & (kt * tq + keyi <= tie) & (key != INT_MIN))
        bias = jnp.where(sel, 0.0, NEG).astype(F32)
        bias = jnp.concatenate([bias, bias], axis=1)
        ss, vts = [], []
        for u in range(n_units):
            ss.append(jnp.dot(ka_ref[pl.ds(start, tq), u * LANES:(u + 1) * LANES], qbd_ref[u],
                              preferred_element_type=F32) + bias)
            vts.append([vat_ref[kt, u * LANES:u * LANES + HEAD_DIM, :],
                        vat_ref[kt, u * LANES + HEAD_DIM:(u + 1) * LANES, :]])
        _flash_units_t(ss, vts, m_ref, l_ref, acc_ref)
        return carry

    lax.fori_loop(0, nkt, kv_tile, 0)
    for u in range(n_units):
        o = _pair_to_rows(acc_ref[u] / l_ref[u], tq) * sg_ref[:, u * LANES:(u + 1) * LANES]
        o_ref[:, u * LANES:(u + 1) * LANES] = o.astype(BF16)


def _dsa_prompt(qat, qit, wit, ki, ka, vat, sg, tq):
    B, _, T = qat.shape
    L = ka.shape[1]
    topk = min(TOPK_MAX, L // 4)
    assert T == L and T % tq == 0 and tq % LANES == 0
    qtmap = lambda b, i: (b, 0, i)
    kmap = lambda b, i: (b, 0, 0)
    return pl.pallas_call(
        functools.partial(_dsa_prompt_kernel, tq=tq, topk=topk, col_bits=max(1, (L - 1).bit_length())),
        grid=(B, T // tq),
        in_specs=[pl.BlockSpec((None, W_A, tq), qtmap), pl.BlockSpec((None, H_IDX * D_IDX, tq), qtmap),
                  pl.BlockSpec((None, H_IDX, tq), qtmap),
                  _resident((None, L, D_IDX), kmap), _resident((None, L, W_A), kmap),
                  _resident((None, L // tq, W_A, tq), lambda b, i: (b, 0, 0, 0)),
                  pl.BlockSpec((None, tq, W_A), lambda b, i: (b, i, 0))],
        out_specs=pl.BlockSpec((None, tq, W_A), lambda b, i: (b, i, 0)),
        out_shape=jax.ShapeDtypeStruct((B, T, W_A), BF16),
        scratch_shapes=[pltpu.VMEM((L, tq), I32)] + _flash_scratch_t(W_A // LANES, HEAD_DIM, tq),
        compiler_params=_cparams("parallel", "arbitrary"),
        name="dsa_prompt",
    )(qat, qit, wit, ki, ka, vat, sg)


def _lambda_value(lam_ref, lam_init):
    lam = lam_ref[...]
    s1 = jnp.sum(lam[0:1] * lam[1:2], axis=1, keepdims=True)
    s2 = jnp.sum(lam[2:3] * lam[3:4], axis=1, keepdims=True)
    return jnp.exp(s1) - jnp.exp(s2) + lam_init


def _diff_finish(acc1, l1, acc2, l2, lam_val, subg, sg, lam_init):
    o = acc1 / l1 - lam_val * (acc2 / l2)
    return _rms(o, subg, SUBLN_EPS) * (1.0 - lam_init) * sg


def _diff_prompt_kernel(qt_ref, k_ref, vt_ref, sg_ref, lam_ref, subg_ref, o_ref, qbd_ref, m_ref, l_ref, acc_ref,
                        *, tq, lam_init):
    i = pl.program_id(1)
    lam_val = _lambda_value(lam_ref, lam_init)
    diag = _causal_bias_t(tq)
    dv = 2 * DH_B
    for h in range(H_B):
        qbd_ref[h] = _block_diag_qt(qt_ref[h * dv:(h + 1) * dv, :], tq)
    _flash_init_t(m_ref, l_ref, acc_ref)

    def tile(kt, carry, bias):
        start = pl.multiple_of(kt * tq, tq)
        ss, vts = [], []
        for h in range(H_B):
            s = jnp.dot(k_ref[pl.ds(start, tq), h * dv:(h + 1) * dv], qbd_ref[h], preferred_element_type=F32)
            ss.append(s if bias is None else s + bias)
            vts.append([vt_ref[kt, h * dv:(h + 1) * dv, :]])
        _flash_units_t(ss, vts, m_ref, l_ref, acc_ref)
        return carry

    lax.fori_loop(0, i, lambda kt, c: tile(kt, c, None), 0)
    tile(i, 0, diag)
    for h in range(H_B):
        o = acc_ref[h] / l_ref[h]
        o = (o[:, :tq] - lam_val * o[:, tq:]).T
        o = _rms(o, subg_ref[...], SUBLN_EPS) * (1.0 - lam_init) * sg_ref[:, h * dv:(h + 1) * dv]
        o_ref[:, h * dv:(h + 1) * dv] = o.astype(BF16)


def _diff_prompt(qbt, kb, vbt, sg, lam, subg, lam_init, tq):
    B, _, T = qbt.shape
    L = kb.shape[1]
    assert T == L and T % tq == 0
    qmap = lambda b, i: (b, i, 0)
    kmap = lambda b, i: (b, 0, 0)
    const = lambda b, i: (0, 0)
    return pl.pallas_call(
        functools.partial(_diff_prompt_kernel, tq=tq, lam_init=lam_init),
        grid=(B, T // tq),
        in_specs=[pl.BlockSpec((None, W_B, tq), lambda b, i: (b, 0, i)), _resident((None, L, W_B), kmap),
                  _resident((None, L // tq, W_B, tq), lambda b, i: (b, 0, 0, 0)),
                  pl.BlockSpec((None, tq, W_B), qmap),
                  pl.BlockSpec(lam.shape, const), pl.BlockSpec(subg.shape, const)],
        out_specs=pl.BlockSpec((None, tq, W_B), qmap),
        out_shape=jax.ShapeDtypeStruct((B, T, W_B), BF16),
        scratch_shapes=_flash_scratch_t(H_B, 2 * DH_B, tq),
        compiler_params=_cparams("parallel", "arbitrary"),
        name="diff_prompt",
    )(qbt, kb, vbt, sg, lam, subg)


def _top_blocks(gate, n_valid, n_sel):
    lane = lax.broadcasted_iota(I32, gate.shape, 1)
    g = jnp.where(lane < n_valid, gate, -jnp.inf)
    sel = jnp.zeros(gate.shape, F32)
    for _ in range(n_sel):
        mx = jnp.max(g, axis=1, keepdims=True)
        first = jnp.min(jnp.where(g == mx, lane, LANES), axis=1, keepdims=True)
        pick = (lane == first) & (lane < n_valid)
        sel = jnp.where(pick, 1.0, sel)
        g = jnp.where(lane == first, -jnp.inf, g)
    return sel


def _top_blocks_t(gate, n_valid, n_sel):
    blk = lax.broadcasted_iota(I32, gate.shape, 0)
    g = jnp.where(blk < n_valid, gate, -jnp.inf)
    sel = jnp.zeros(gate.shape, F32)
    for _ in range(n_sel):
        mx = jnp.max(g, axis=0, keepdims=True)
        first = jnp.min(jnp.where(g == mx, blk, LANES), axis=0, keepdims=True)
        pick = (blk == first) & (blk < n_valid)
        sel = jnp.where(pick, 1.0, sel)
        g = jnp.where(blk == first, -jnp.inf, g)
    return sel


def _moba_prompt_kernel(qt_ref, kmean_ref, k_ref, vt_ref, sg_ref, o_ref, bias_ref, qbd_ref, m_ref, l_ref, acc_ref,
                        *, tq, n_sel, n_units):
    i = pl.program_id(2)
    diag = _causal_bias_t(tq)
    for u in range(n_units):
        qbd = _block_diag_qt(qt_ref[u * LANES:(u + 1) * LANES, :], tq)
        gate = jnp.dot(kmean_ref[:, u * LANES:(u + 1) * LANES].astype(BF16), qbd,
                       preferred_element_type=F32)
        bias_ref[u] = (_top_blocks_t(gate, i, n_sel) - 1.0) * (-NEG)
        qbd_ref[u] = qbd
    _flash_init_t(m_ref, l_ref, acc_ref)

    def tile(kt, carry, diagonal):
        start = pl.multiple_of(kt * tq, tq)
        ss, vts = [], []
        for u in range(n_units):
            bias = diag if diagonal else bias_ref[u, pl.ds(kt, 1), :]
            ss.append(jnp.dot(k_ref[pl.ds(start, tq), u * LANES:(u + 1) * LANES], qbd_ref[u],
                              preferred_element_type=F32) + bias)
            vts.append([vt_ref[kt, u * LANES:u * LANES + HEAD_DIM, :],
                        vt_ref[kt, u * LANES + HEAD_DIM:(u + 1) * LANES, :]])
        _flash_units_t(ss, vts, m_ref, l_ref, acc_ref)
        return carry

    lax.fori_loop(0, i, lambda kt, c: tile(kt, c, False), 0)
    tile(i, 0, True)
    for u in range(n_units):
        o = _pair_to_rows(acc_ref[u] / l_ref[u], tq) * sg_ref[:, u * LANES:(u + 1) * LANES]
        o_ref[:, u * LANES:(u + 1) * LANES] = o.astype(BF16)


def _moba_prompt(qt, kmean, k, vt, sg, hg=8):
    B, _, T = qt.shape
    L = k.shape[1]
    tq = MOBA_BLOCK
    nb = L // tq
    assert T == L and L % tq == 0 and nb <= LANES
    n_sel = max(1, min(MOBA_TOPB, nb - 1))
    wg = hg * HEAD_DIM
    n_units = wg // LANES
    qmap = lambda b, g, i: (b, i, g)
    kmap = lambda b, g, i: (b, 0, g)
    return pl.pallas_call(
        functools.partial(_moba_prompt_kernel, tq=tq, n_sel=n_sel, n_units=n_units),
        grid=(B, W_C // wg, T // tq),
        in_specs=[pl.BlockSpec((None, wg, tq), lambda b, g, i: (b, g, i)),
                  _resident((None, LANES, wg), kmap), _resident((None, L, wg), kmap),
                  _resident((None, nb, wg, tq), lambda b, g, i: (b, 0, g, 0)),
                  pl.BlockSpec((None, tq, wg), qmap)],
        out_specs=pl.BlockSpec((None, tq, wg), qmap),
        out_shape=jax.ShapeDtypeStruct((B, T, W_C), BF16),
        scratch_shapes=[pltpu.VMEM((n_units, LANES, 2 * tq), F32)] + _flash_scratch_t(n_units, HEAD_DIM, tq),
        compiler_params=_cparams("parallel", "parallel", "arbitrary"),
        name="moba_prompt",
    )(qt, kmean, k, vt, sg)


def _pad_rows(a, rows):
    return jnp.concatenate([a, jnp.zeros((rows - a.shape[0], a.shape[1]), a.dtype)], axis=0)


def _new_token_bias(nq):
    r = lax.broadcasted_iota(I32, (nq, LANES), 0)
    c = lax.broadcasted_iota(I32, (nq, LANES), 1)
    return jnp.where(c <= r, 0.0, NEG).astype(F32)


def _fold_heads(o, n_heads, nq, width):
    col_head = lax.broadcasted_iota(I32, (nq, n_heads * width), 1) // width
    out = jnp.zeros((nq, n_heads * width), F32)
    for h in range(n_heads):
        out = jnp.where(col_head == h, o[h * nq:(h + 1) * nq, :], out)
    return out


def _pages_per_step(n_pages, cap):
    pps = cap
    while n_pages % pps:
        pps //= 2
    return pps


def _page_specs(block, layer, pps):
    def spec(j):
        def index_map(b, s, pt):
            return (pt[b, s * pps + j], layer) + (0,) * (len(block) - 2)
        return pl.BlockSpec(block, index_map)
    return [spec(j) for j in range(pps)]


def _paged_step(q, ks, vs, bias, m_ref, l_ref, acc_ref, transposed):
    nt = (((1,), (1,)), ((), ()))
    cat = lambda xs, axis: jnp.concatenate(xs, axis=axis) if len(xs) > 1 else xs[0]
    if transposed:
        s = jnp.dot(q, cat(ks, 1), preferred_element_type=F32)
    else:
        s = lax.dot_general(q, cat(ks, 0), nt, preferred_element_type=F32)
    if bias is not None:
        s = s + bias
    m_prev = m_ref[...]
    m_new = jnp.maximum(m_prev, jnp.max(s, axis=1, keepdims=True))
    alpha = jnp.exp(m_prev - m_new)
    p = jnp.exp(s - m_new)
    l_ref[...] = alpha * l_ref[...] + jnp.sum(p, axis=1, keepdims=True)
    pb = p.astype(BF16)
    if transposed:
        pv = lax.dot_general(pb, cat(vs, 1), nt, preferred_element_type=F32)
    else:
        pv = jnp.dot(pb, cat(vs, 0), preferred_element_type=F32)
    acc_ref[...] = alpha * acc_ref[...] + pv
    m_ref[...] = m_new


def _dsa_select_paged_kernel(pt_ref, qi_ref, wi_ref, *refs, n_pages, pps, nq, topk, col_bits):
    kc_refs = refs[:pps]
    kn_ref, bias_ref, key_ref, thr_ref, tie_ref = refs[pps:]
    s = pl.program_id(1)
    qi = qi_ref[...]
    wi = wi_ref[...]

    def score(rel):
        wr = wi * jnp.maximum(rel, 0.0)
        out = jnp.zeros((nq, rel.shape[1]), F32)
        for h in range(H_IDX):
            out = out + wr[h * nq:(h + 1) * nq, :]
        return out

    for j in range(pps):
        rel = jnp.dot(qi, kc_refs[j][...].astype(BF16), preferred_element_type=F32)
        key_ref[s * pps + j] = _sortable_key(score(rel))

    @pl.when(s == n_pages // pps - 1)
    def _():
        kn = _pad_rows(kn_ref[...], LANES).astype(BF16)
        sc = score(lax.dot_general(qi, kn, (((1,), (1,)), ((), ())), preferred_element_type=F32))
        r = lax.broadcasted_iota(I32, (nq, LANES), 0)
        lane = lax.broadcasted_iota(I32, (nq, LANES), 1)
        key_ref[n_pages] = jnp.where(lane <= r, _sortable_key(sc), INT_MIN)
        _topk_select(key_ref, thr_ref, tie_ref, n_pages + 1, nq, topk, col_bits, rb=nq)
        thr = thr_ref[...]
        tie = tie_ref[...]

        def bias_chunk(c, carry):
            bias_ref[c] = _select_bias(key_ref[c], thr, tie, c * LANES + lane)
            return carry

        lax.fori_loop(0, n_pages + 1, bias_chunk, 0)


def _dsa_select_paged(page_table, layer, qi_s, wi_s, cache_idx_t, ki_new):
    Bd, n_pages = page_table.shape
    nq = ki_new.shape[1]
    L = n_pages * LANES + nq
    topk = min(TOPK_MAX, L // 4)
    pps = _pages_per_step(n_pages, 8)
    assert cache_idx_t.shape[3] == LANES and nq == 8
    bmap = lambda b, s, pt: (b, 0, 0)
    return pl.pallas_call(
        functools.partial(_dsa_select_paged_kernel, n_pages=n_pages, pps=pps, nq=nq, topk=topk,
                          col_bits=max(1, (L - 1).bit_length())),
        grid_spec=pltpu.PrefetchScalarGridSpec(
            num_scalar_prefetch=1, grid=(Bd, n_pages // pps),
            in_specs=[pl.BlockSpec((None,) + qi_s.shape[1:], bmap), pl.BlockSpec((None,) + wi_s.shape[1:], bmap)]
                     + _page_specs((None, None, D_IDX, LANES), layer, pps)
                     + [pl.BlockSpec((None, nq, D_IDX), bmap)],
            out_specs=pl.BlockSpec((None, n_pages + 1, nq, LANES), lambda b, s, pt: (b, 0, 0, 0)),
            scratch_shapes=[pltpu.VMEM((n_pages + 1, nq, LANES), I32), pltpu.VMEM((nq, LANES), I32),
                            pltpu.VMEM((nq, LANES), I32)]),
        out_shape=jax.ShapeDtypeStruct((Bd, n_pages + 1, nq, LANES), F32),
        compiler_params=_cparams("parallel", "arbitrary"),
        name="dsa_sample_select",
    )(page_table, qi_s, wi_s, *([cache_idx_t] * pps), ki_new)


def _dsa_attn_paged_kernel(pt_ref, q_ref, bias_ref, biasn_ref, *refs, n_pages, pps, nq):
    kv_refs = refs[:pps]
    kvn_ref, sg_ref, o_ref, m_ref, l_ref, acc_ref = refs[pps:]
    s = pl.program_id(1)
    q = q_ref[...]

    @pl.when(s == 0)
    def _():
        _flash_init_t(m_ref, l_ref, acc_ref)

    ks = [kv_refs[j][0].reshape(W_A, LANES).astype(BF16) for j in range(pps)]
    vs = [kv_refs[j][1].reshape(W_A, LANES).astype(BF16) for j in range(pps)]
    bias = jnp.concatenate([jnp.concatenate([bias_ref[j]] * H_A, axis=0) for j in range(pps)], axis=1)
    _paged_step(q, ks, vs, bias, m_ref, l_ref, acc_ref, transposed=True)

    @pl.when(s == n_pages // pps - 1)
    def _():
        kv = _pad_rows(kvn_ref[...], LANES).astype(BF16)
        bias_n = jnp.concatenate([biasn_ref[...]] * H_A, axis=0)
        _paged_step(q, [kv[:, 0:W_A]], [kv[:, W_A:2 * W_A]], bias_n, m_ref, l_ref, acc_ref, transposed=False)
        o = _fold_heads(acc_ref[...] / l_ref[...], H_A, nq, HEAD_DIM)
        o_ref[...] = (o * sg_ref[...]).astype(BF16)


def _dsa_attn_paged(page_table, layer, q_bd, cache_kv_t, kv_new, bias, sg):
    Bd, n_pages = page_table.shape
    nq = kv_new.shape[1]
    rows = q_bd.shape[1]
    pps = _pages_per_step(n_pages, 8)
    bmap = lambda b, s, pt: (b, 0, 0)
    return pl.pallas_call(
        functools.partial(_dsa_attn_paged_kernel, n_pages=n_pages, pps=pps, nq=nq),
        grid_spec=pltpu.PrefetchScalarGridSpec(
            num_scalar_prefetch=1, grid=(Bd, n_pages // pps),
            in_specs=[pl.BlockSpec((None, rows, W_A), bmap),
                      pl.BlockSpec((None, pps, nq, LANES), lambda b, s, pt: (b, s, 0, 0)),
                      pl.BlockSpec((None, None, nq, LANES), lambda b, s, pt: (b, n_pages, 0, 0))]
                     + _page_specs((None, None, 2, H_A, HEAD_DIM, LANES), layer, pps)
                     + [pl.BlockSpec((None, nq, 2 * W_A), bmap), pl.BlockSpec((None, nq, W_A), bmap)],
            out_specs=pl.BlockSpec((None, nq, W_A), bmap),
            scratch_shapes=[pltpu.VMEM((rows, 1), F32), pltpu.VMEM((rows, 1), F32), pltpu.VMEM((rows, W_A), F32)]),
        out_shape=jax.ShapeDtypeStruct((Bd, nq, W_A), BF16),
        compiler_params=_cparams("parallel", "arbitrary"),
        name="dsa_sample_attn",
    )(page_table, q_bd, bias, bias, *([cache_kv_t] * pps), kv_new, sg)


def _diff_paged_kernel(pt_ref, q_ref, *refs, n_pages, pps, nq, lam_init):
    kv_refs = refs[:pps]
    kvn_ref, sg_ref, lam_ref, subg_ref, o_ref, m_ref, l_ref, acc_ref = refs[pps:]
    s = pl.program_id(1)
    dv = 2 * DH_B
    rows_per_slot = 2 * H_B

    @pl.when(s == 0)
    def _():
        _flash_init_t(m_ref, l_ref, acc_ref)

    for h in range(H_B):
        ks = [kv_refs[j][pl.ds(h, LANES, stride=rows_per_slot), :].astype(BF16) for j in range(pps)]
        vs = [kv_refs[j][pl.ds(H_B + h, LANES, stride=rows_per_slot), :].astype(BF16) for j in range(pps)]
        _paged_step(q_ref[h], ks, vs, None, m_ref.at[h], l_ref.at[h], acc_ref.at[h], transposed=False)

    @pl.when(s == n_pages // pps - 1)
    def _():
        kv = _pad_rows(kvn_ref[...], LANES).astype(BF16)
        bias = jnp.concatenate([_new_token_bias(nq)] * 2, axis=0)
        lam_val = _lambda_value(lam_ref, lam_init)
        for h in range(H_B):
            _paged_step(q_ref[h], [kv[:, h * dv:(h + 1) * dv]], [kv[:, W_B + h * dv:W_B + (h + 1) * dv]], bias,
                        m_ref.at[h], l_ref.at[h], acc_ref.at[h], transposed=False)
            l = l_ref[h]
            o = _diff_finish(acc_ref[h, 0:nq, :], l[0:nq], acc_ref[h, nq:2 * nq, :], l[nq:2 * nq],
                             lam_val, subg_ref[...], sg_ref[:, h * dv:(h + 1) * dv], lam_init)
            o_ref[:, h * dv:(h + 1) * dv] = o.astype(BF16)


def _diff_paged(page_table, layer, q_h, cache_rows, kv_new, sg, lam, subg, lam_init):
    Bd, n_pages = page_table.shape
    nq = kv_new.shape[1]
    dv = 2 * DH_B
    pps = _pages_per_step(n_pages, 8)
    bmap = lambda b, s, pt: (b, 0, 0)
    const = lambda b, s, pt: (0, 0)
    return pl.pallas_call(
        functools.partial(_diff_paged_kernel, n_pages=n_pages, pps=pps, nq=nq, lam_init=lam_init),
        grid_spec=pltpu.PrefetchScalarGridSpec(
            num_scalar_prefetch=1, grid=(Bd, n_pages // pps),
            in_specs=[pl.BlockSpec((None, H_B, 2 * nq, dv), lambda b, s, pt: (b, 0, 0, 0))]
                     + _page_specs((None, None, LANES * 2 * H_B, dv), layer, pps)
                     + [pl.BlockSpec((None, nq, 2 * W_B), bmap), pl.BlockSpec((None, nq, W_B), bmap),
                        pl.BlockSpec(lam.shape, const), pl.BlockSpec(subg.shape, const)],
            out_specs=pl.BlockSpec((None, nq, W_B), bmap),
            scratch_shapes=[pltpu.VMEM((H_B, 2 * nq, 1), F32), pltpu.VMEM((H_B, 2 * nq, 1), F32),
                            pltpu.VMEM((H_B, 2 * nq, dv), F32)]),
        out_shape=jax.ShapeDtypeStruct((Bd, nq, W_B), BF16),
        compiler_params=_cparams("parallel", "arbitrary"),
        name="diff_sample",
    )(page_table, q_h, *([cache_rows] * pps), kv_new, sg, lam, subg)


def _moba_gate_paged_kernel(pt_ref, q_ref, *refs, n_pages, pps, ppb, n_sel):
    kc_refs = refs[:pps]
    sel_ref, gate_ref = refs[pps:]
    s = pl.program_id(1)
    q = q_ref[...]

    @pl.when(s == 0)
    def _():
        gate_ref[...] = jnp.zeros(gate_ref.shape, F32)

    lane = lax.broadcasted_iota(I32, gate_ref.shape, 1)
    for bi in range(pps // ppb):
        tot = None
        for j in range(bi * ppb, (bi + 1) * ppb):
            sc = jnp.dot(q, kc_refs[j][...].reshape(W_C, LANES).astype(BF16), preferred_element_type=F32)
            rs = jnp.sum(sc, axis=1, keepdims=True)
            tot = rs if tot is None else tot + rs
        gate_ref[...] = jnp.where(lane == s * (pps // ppb) + bi, tot * (1.0 / MOBA_BLOCK), gate_ref[...])

    @pl.when(s == n_pages // pps - 1)
    def _():
        sel_ref[...] = _top_blocks(gate_ref[...], n_pages // ppb, n_sel)


def _moba_gate_paged(page_table, layer, q_bd, cache_kv_t):
    Bd, n_pages = page_table.shape
    rows = q_bd.shape[1]
    ppb = MOBA_BLOCK // LANES
    nbp = n_pages // ppb
    pps = _pages_per_step(n_pages, 8)
    assert n_pages % ppb == 0 and pps % ppb == 0 and nbp <= LANES
    n_sel = max(1, min(MOBA_TOPB, nbp))
    bmap = lambda b, s, pt: (b, 0, 0)

    def kspec(j):
        def index_map(b, s, pt):
            return (pt[b, s * pps + j], layer, 0, 0, 0, 0)
        return pl.BlockSpec((None, None, None, H_C, HEAD_DIM, LANES), index_map)

    return pl.pallas_call(
        functools.partial(_moba_gate_paged_kernel, n_pages=n_pages, pps=pps, ppb=ppb, n_sel=n_sel),
        grid_spec=pltpu.PrefetchScalarGridSpec(
            num_scalar_prefetch=1, grid=(Bd, n_pages // pps),
            in_specs=[pl.BlockSpec((None, rows, W_C), bmap)] + [kspec(j) for j in range(pps)],
            out_specs=pl.BlockSpec((None, rows, LANES), bmap),
            scratch_shapes=[pltpu.VMEM((rows, LANES), F32)]),
        out_shape=jax.ShapeDtypeStruct((Bd, rows, LANES), F32),
        compiler_params=_cparams("parallel", "arbitrary"),
        name="moba_sample_gate",
    )(page_table, q_bd, *([cache_kv_t] * pps))


def _moba_attn_paged_kernel(pt_ref, q_ref, sel_ref, *refs, n_pages, pps, ppb, nq):
    kv_refs = refs[:pps]
    kvn_ref, sg_ref, o_ref, m_ref, l_ref, acc_ref = refs[pps:]
    s = pl.program_id(1)
    q = q_ref[...]
    blk_row = lax.broadcasted_iota(I32, (LANES, LANES), 0)

    @pl.when(s == 0)
    def _():
        _flash_init_t(m_ref, l_ref, acc_ref)

    sel = sel_ref[...].astype(BF16)
    bias = []
    for j in range(pps):
        onehot = jnp.where(blk_row == (s * pps + j) // ppb, 1.0, 0.0).astype(BF16)
        picked = jnp.dot(sel, onehot, preferred_element_type=F32)
        bias.append((picked - 1.0) * (-NEG))
    ks = [kv_refs[j][0].reshape(W_C, LANES).astype(BF16) for j in range(pps)]
    vs = [kv_refs[j][1].reshape(W_C, LANES).astype(BF16) for j in range(pps)]
    _paged_step(q, ks, vs, jnp.concatenate(bias, axis=1), m_ref, l_ref, acc_ref, transposed=True)

    @pl.when(s == n_pages // pps - 1)
    def _():
        kv = _pad_rows(kvn_ref[...], LANES).astype(BF16)
        bias_n = jnp.concatenate([_new_token_bias(nq)] * H_C, axis=0)
        _paged_step(q, [kv[:, 0:W_C]], [kv[:, W_C:2 * W_C]], bias_n, m_ref, l_ref, acc_ref, transposed=False)
        o = _fold_heads(acc_ref[...] / l_ref[...], H_C, nq, HEAD_DIM)
        o_ref[...] = (o * sg_ref[...]).astype(BF16)


def _moba_attn_paged(page_table, layer, q_bd, cache_kv_t, kv_new, sel, sg):
    Bd, n_pages = page_table.shape
    nq = kv_new.shape[1]
    rows = q_bd.shape[1]
    pps = _pages_per_step(n_pages, 8)
    bmap = lambda b, s, pt: (b, 0, 0)
    return pl.pallas_call(
        functools.partial(_moba_attn_paged_kernel, n_pages=n_pages, pps=pps, ppb=MOBA_BLOCK // LANES, nq=nq),
        grid_spec=pltpu.PrefetchScalarGridSpec(
            num_scalar_prefetch=1, grid=(Bd, n_pages // pps),
            in_specs=[pl.BlockSpec((None, rows, W_C), bmap), pl.BlockSpec((None, rows, LANES), bmap)]
                     + _page_specs((None, None, 2, H_C, HEAD_DIM, LANES), layer, pps)
                     + [pl.BlockSpec((None, nq, 2 * W_C), bmap), pl.BlockSpec((None, nq, W_C), bmap)],
            out_specs=pl.BlockSpec((None, nq, W_C), bmap),
            scratch_shapes=[pltpu.VMEM((rows, 1), F32), pltpu.VMEM((rows, 1), F32), pltpu.VMEM((rows, W_C), F32)]),
        out_shape=jax.ShapeDtypeStruct((Bd, nq, W_C), BF16),
        compiler_params=_cparams("parallel", "arbitrary"),
        name="moba_sample_attn",
    )(page_table, q_bd, sel, *([cache_kv_t] * pps), kv_new, sg)


def _block_diag_queries(q2, Bd, nq, n_heads, width):
    q4 = q2.reshape(Bd, nq, n_heads, width)
    eye = jnp.eye(n_heads, dtype=q2.dtype)
    return jnp.einsum('bqhd,hg->bhqgd', q4, eye).reshape(Bd, n_heads * nq, n_heads * width)


def _sample_trunk(x, past_len, caches, page_table, norm_g, w_even, w_out_even, lam_even, subln_g_even,
                  w_odd, w_out_odd, final_norm_g):
    cache_a_kv, cache_a_idx, cache_b_kv, cache_c_kv = caches
    Bd, nq, D = x.shape
    M = Bd * nq
    n_pool, _, page, _ = cache_a_idx.shape
    assert page == LANES and past_len % MOBA_BLOCK == 0 and nq <= 8
    cb_rows = cache_b_kv.reshape(n_pool, cache_b_kv.shape[1], page * 2 * H_B, 2 * DH_B)
    ca_kv_t = cache_a_kv.transpose(0, 1, 3, 4, 5, 2)
    ca_idx_t = cache_a_idx.transpose(0, 1, 3, 2)
    cc_kv_t = cache_c_kv.transpose(0, 1, 3, 4, 5, 2)
    depth = norm_g.shape[0]
    heads_first = lambda a, nh: a.reshape(Bd, nq, nh, HEAD_DIM).transpose(0, 2, 1, 3).reshape(Bd, nh * nq, HEAD_DIM)
    cos, sin = _rope_tables(past_len + jnp.arange(nq, dtype=I32))
    cos, sin = jnp.tile(cos, (Bd, 1)), jnp.tile(sin, (Bd, 1))
    x2 = x.reshape(M, D)
    a_kv, a_idx, b_kv, c_kv = [], [], [], []
    y = None
    r3 = lambda a: a.reshape(Bd, nq, a.shape[1])
    for li in range(depth):
        j = li // 2
        g = norm_g[li][None, :]
        last = li == depth - 1
        gf = final_norm_g[None, :] if last else None
        if li % 2 == 0:
            lam_init = 0.8 - 0.6 * math.exp(-0.3 * li)
            (qa, kva, _, _, sg, qi, kif, _, wi, qb, kvb, _, _) = _even_proj(x2, g, *w_even[j], cos, sin, M)
            wi_s = wi.reshape(Bd, nq, H_IDX).transpose(0, 2, 1).reshape(Bd, H_IDX * nq, 1)
            bias = _dsa_select_paged(page_table, j, heads_first(qi, H_IDX), wi_s, ca_idx_t, r3(kif))
            oa = _dsa_attn_paged(page_table, j, _block_diag_queries(qa, Bd, nq, H_A, HEAD_DIM), ca_kv_t,
                                 r3(kva), bias, r3(sg[:, :W_A]))
            qb_h = jnp.einsum('bqhmd,mn->bhmqnd', qb.reshape(Bd, nq, H_B, 2, DH_B), jnp.eye(2, dtype=qb.dtype))
            ob = _diff_paged(page_table, j, qb_h.reshape(Bd, H_B, 2 * nq, 2 * DH_B), cb_rows, r3(kvb),
                             r3(sg[:, W_A:]), lam_even[j], subln_g_even[j][None, :], lam_init)
            res = _out_proj([oa.reshape(M, W_A), ob.reshape(M, W_B)], w_out_even[j], x2, gf, M)
            a_kv.append(kva.reshape(Bd, nq, 2, H_A, HEAD_DIM))
            a_idx.append(kif.reshape(Bd, nq, D_IDX))
            b_kv.append(kvb.reshape(Bd, nq, 2, H_B, 2 * DH_B))
        else:
            q, kv, _, _, sg, _ = _odd_proj(x2, g, w_odd[j], cos, sin, M)
            q_bd = _block_diag_queries(q, Bd, nq, H_C, HEAD_DIM)
            sel = _moba_gate_paged(page_table, j, q_bd, cc_kv_t)
            o = _moba_attn_paged(page_table, j, q_bd, cc_kv_t, r3(kv), sel, r3(sg))
            res = _out_proj([o.reshape(M, W_C)], w_out_odd[j], x2, gf, M)
            c_kv.append(kv.reshape(Bd, nq, 2, H_C, HEAD_DIM))
        x2 = res[0]
        if last:
            y = res[1]
    return (y.reshape(Bd, nq, D), jnp.stack(a_kv, axis=1), jnp.stack(a_idx, axis=1),
            jnp.stack(b_kv, axis=1), jnp.stack(c_kv, axis=1))


def _rope_tables(pos):
    half = HEAD_DIM // 2
    inv_freq = jnp.exp(-math.log(ROPE_THETA) * jnp.arange(half, dtype=F32) / half)
    ang = pos.astype(F32)[:, None] * inv_freq[None, :]
    cos, sin = jnp.cos(ang), jnp.sin(ang)
    return jnp.concatenate([cos, cos, cos, cos], axis=1), jnp.concatenate([-sin, sin, -sin, sin], axis=1)


def _split_even_weight(w):
    wa = w[:, 0:4 * W_A].astype(BF16)
    wi = jnp.pad(w[:, 4 * W_A:4 * W_A + IDX_COLS], ((0, 0), (0, IDX_PAD - IDX_COLS))).astype(BF16)
    wb = w[:, 4 * W_A + IDX_COLS:].astype(BF16)
    return wa, wi, wb


def _prompt_trunk(x, norm_g, w_even, w_out_even, lam_even, subln_g_even, w_odd, w_out_odd, final_norm_g):
    B, T, D = x.shape
    M = B * T
    tm = min(256, T)
    depth = norm_g.shape[0]
    cos, sin = _rope_tables(jnp.arange(T, dtype=I32))
    cos, sin = jnp.tile(cos, (B, 1)), jnp.tile(sin, (B, 1))
    x2 = x.reshape(M, D)
    a_kv, a_idx, b_kv, c_kv = [], [], [], []
    y = None
    r3 = lambda a: a.reshape(B, T, a.shape[1])
    for li in range(depth):
        j = li // 2
        g = norm_g[li][None, :]
        last = li == depth - 1
        gf = final_norm_g[None, :] if last else None
        if li % 2 == 0:
            lam_init = 0.8 - 0.6 * math.exp(-0.3 * li)
            (qat, kva, ka, vat, sg, qit, kif, kib, wi, qbt, kvb, kb, vbt) = _even_proj(
                x2, g, *w_even[j], cos, sin, tm, bt=(B, T))
            oa = _dsa_prompt(qat, qit, r3(wi).transpose(0, 2, 1), r3(kib), r3(ka), vat, r3(sg[:, :W_A]), tm)
            ob = _diff_prompt(qbt, r3(kb), vbt, r3(sg[:, W_A:]), lam_even[j],
                              subln_g_even[j][None, :], lam_init, tm)
            res = _out_proj([oa.reshape(M, W_A), ob.reshape(M, W_B)], w_out_even[j], x2, gf, tm)
            a_kv.append(kva.reshape(B, T, 2, H_A, HEAD_DIM))
            a_idx.append(kif.reshape(B, T, D_IDX))
            b_kv.append(kvb.reshape(B, T, 2, H_B, 2 * DH_B))
        else:
            assert T % MOBA_BLOCK == 0
            qt, kv, k, vt, sg, ksum = _odd_proj(x2, g, w_odd[j], cos, sin, MOBA_BLOCK, bt=(B, T))
            nb = T // MOBA_BLOCK
            kmean = ksum.reshape(B, nb, W_C) * (1.0 / MOBA_BLOCK)
            kmean = jnp.pad(kmean, ((0, 0), (0, LANES - nb), (0, 0)))
            o = _moba_prompt(qt, kmean, r3(k), vt, r3(sg))
            res = _out_proj([o.reshape(M, W_C)], w_out_odd[j], x2, gf, tm)
            c_kv.append(kv.reshape(B, T, 2, H_C, HEAD_DIM))
        x2 = res[0]
        if last:
            y = res[1]
    return (y.reshape(B, T, D), jnp.stack(a_kv, axis=1), jnp.stack(a_idx, axis=1),
            jnp.stack(b_kv, axis=1), jnp.stack(c_kv, axis=1))


def kernel(x_prompt, x_sample, cache_a_kv, cache_a_idx, cache_b_kv, cache_c_kv, page_table, norm_g,
           w_in_even, w_out_even, lam_even, subln_g_even, w_in_odd, w_out_odd, final_norm_g):
    w_even = [_split_even_weight(w_in_even[j]) for j in range(w_in_even.shape[0])]
    w_odd = [w_in_odd[j].astype(BF16) for j in range(w_in_odd.shape[0])]
    w_out_e = [w_out_even[j].astype(BF16) for j in range(w_out_even.shape[0])]
    w_out_o = [w_out_odd[j].astype(BF16) for j in range(w_out_odd.shape[0])]
    weights = (norm_g, w_even, w_out_e, lam_even, subln_g_even, w_odd, w_out_o, final_norm_g)
    y_p, a_kv_p, a_idx_p, b_kv_p, c_kv_p = _prompt_trunk(x_prompt, *weights)
    past_len = page_table.shape[1] * cache_a_idx.shape[2]
    y_s, a_kv_s, a_idx_s, b_kv_s, c_kv_s = _sample_trunk(
        x_sample, past_len, (cache_a_kv, cache_a_idx, cache_b_kv, cache_c_kv), page_table, *weights)
    return (y_p, y_s, a_kv_p, a_idx_p, b_kv_p, c_kv_p, a_kv_s, a_idx_s, b_kv_s, c_kv_s)
```
